```python
import jax, jax.numpy as jnp
from jax import lax
import numpy as np


D_MODEL = 1024
BATCH = 4
SEQ = 8192
DEPTH = 4

N_MIXERS = 2
N_ATTN_LAYERS = (DEPTH + 1) // 2
N_POOL_LAYERS = DEPTH // 2
N_HEADS = 16
QK_NOPE_DIM = 64
QK_ROPE_DIM = 32
V_HEAD_DIM = 64
Q_LORA_RANK = 256
KV_LORA_RANK = 128
ROPE_THETA = 10000.0
Q_BLOCK = 128
POOL_WINDOWS = (2, 4, 8, 16)
N_POOL_GROUPS = 4
POOL_GROUP_DIM = D_MODEL // N_POOL_GROUPS
N_EXPERT_GROUPS = 8
EXPERTS_PER_GROUP = 8
N_EXPERTS = N_EXPERT_GROUPS * EXPERTS_PER_GROUP
TOP_K_IN_GROUP = 2
D_EXPERT = 256
EXPERT_BLOCK = 128
DEEPNORM_ALPHA = (2 * DEPTH) ** 0.25
DEEPNORM_BETA = (8 * DEPTH) ** -0.25
LN_EPS = 1e-5
RMS_EPS = 1e-6

kernel_name = 'mla_pool_hier_moe_deepnorm'


def layer_norm(x, g, b):
    xf = x.astype(jnp.float32)
    mu = jnp.mean(xf, axis=-1, keepdims=True)
    var = jnp.mean(jnp.square(xf - mu), axis=-1, keepdims=True)
    y = (xf - mu) * lax.rsqrt(var + LN_EPS) * g.astype(jnp.float32) + b.astype(jnp.float32)
    return y.astype(x.dtype)


def rms_norm(x, g):
    xf = x.astype(jnp.float32)
    y = xf * lax.rsqrt(jnp.mean(jnp.square(xf), axis=-1, keepdims=True) + RMS_EPS)
    return (y * g.astype(jnp.float32)).astype(x.dtype)


def rope_tables(positions):
    inv_freq = ROPE_THETA ** (-jnp.arange(0, QK_ROPE_DIM, 2, dtype=jnp.float32) / QK_ROPE_DIM)
    ang = positions.astype(jnp.float32)[..., None] * inv_freq
    return jnp.cos(ang), jnp.sin(ang)


def apply_rope(x, cos, sin):
    xf = x.astype(jnp.float32)
    half = xf.shape[-1] // 2
    x1, x2 = xf[..., :half], xf[..., half:]
    y = jnp.concatenate([x1 * cos - x2 * sin, x2 * cos + x1 * sin], axis=-1)
    return y.astype(x.dtype)


def mla_mixer(x, cos, sin, w_in, q_norm, w_uq, kv_norm, w_ukv, w_o):
    B, S, _ = x.shape
    lat = x @ w_in
    c_q = rms_norm(lat[..., :Q_LORA_RANK], q_norm)
    c_kv = rms_norm(lat[..., Q_LORA_RANK:Q_LORA_RANK + KV_LORA_RANK], kv_norm)
    k_rope = apply_rope(lat[..., Q_LORA_RANK + KV_LORA_RANK:], cos, sin)
    q = (c_q @ w_uq).reshape(B, S, N_HEADS, QK_NOPE_DIM + QK_ROPE_DIM)
    q_nope = q[..., :QK_NOPE_DIM]
    q_rope = apply_rope(q[..., QK_NOPE_DIM:], cos[:, :, None, :], sin[:, :, None, :])
    kv = (c_kv @ w_ukv).reshape(B, S, N_HEADS, QK_NOPE_DIM + V_HEAD_DIM)
    k_nope, v = kv[..., :QK_NOPE_DIM], kv[..., QK_NOPE_DIM:]
    scale = (QK_NOPE_DIM + QK_ROPE_DIM) ** -0.5
    outs = []
    for qs in range(0, S, Q_BLOCK):
        qe = qs + Q_BLOCK
        s = (jnp.einsum('bqhd,bkhd->bhqk', q_nope[:, qs:qe], k_nope[:, :qe],
                        preferred_element_type=jnp.float32)
             + jnp.einsum('bqhr,bkr->bhqk', q_rope[:, qs:qe], k_rope[:, :qe],
                          preferred_element_type=jnp.float32)) * scale
        causal = jnp.arange(qe)[None, :] <= jnp.arange(qs, qe)[:, None]
        s = jnp.where(causal, s, -jnp.inf)
        p = jax.nn.softmax(s, axis=-1).astype(v.dtype)
        outs.append(jnp.einsum('bhqk,bkhd->bqhd', p, v[:, :qe]))
    o = jnp.concatenate(outs, axis=1).reshape(B, S, N_HEADS * V_HEAD_DIM)
    return o @ w_o


def pool_mixer(x, w, b, scale):
    B, S, D = x.shape
    xf = x.astype(jnp.float32)
    cs = jnp.concatenate([jnp.zeros((B, 1, D), jnp.float32), jnp.cumsum(xf, axis=1)], axis=1)
    t = jnp.arange(S)
    means = []
    for gi, win in enumerate(POOL_WINDOWS):
        sl = slice(gi * POOL_GROUP_DIM, (gi + 1) * POOL_GROUP_DIM)
        hi = cs[:, 1:, sl]
        lo = cs[:, jnp.maximum(t + 1 - win, 0), sl]
        cnt = jnp.minimum(t + 1, win).astype(jnp.float32)[:, None]
        means.append((hi - lo) / cnt)
    pooled = jnp.concatenate(means, axis=-1) - xf
    y = jnp.einsum('bsgc,gcd->bsgd', pooled.reshape(B, S, N_POOL_GROUPS, POOL_GROUP_DIM),
                   w.astype(jnp.float32)).reshape(B, S, D) + b.astype(jnp.float32)
    return (y * scale.astype(jnp.float32)).astype(x.dtype)


def hier_moe(h, w_grp, b_grp, w_exp, b_exp, w1, w3, w2):
    B, S, D = h.shape
    tok = h.reshape(-1, D)
    N = tok.shape[0]
    hf = tok.astype(jnp.float32)
    g_logits = hf @ w_grp.astype(jnp.float32) + b_grp.astype(jnp.float32)
    g_prob = jax.nn.softmax(g_logits, axis=-1)
    g_sel = jnp.argmax(g_logits, axis=-1)
    g_gate = jnp.take_along_axis(g_prob, g_sel[:, None], axis=-1)
    e_logits = (hf @ w_exp.astype(jnp.float32) + b_exp.astype(jnp.float32)).reshape(N, N_EXPERT_GROUPS, EXPERTS_PER_GROUP)
    e_logits = jnp.take_along_axis(e_logits, g_sel[:, None, None], axis=1)[:, 0]
    e_prob = jax.nn.softmax(e_logits, axis=-1)
    top_p, top_i = lax.top_k(e_prob, TOP_K_IN_GROUP)
    gate = g_gate * top_p / jnp.sum(top_p, axis=-1, keepdims=True)
    expert = g_sel[:, None] * EXPERTS_PER_GROUP + top_i
    A = N * TOP_K_IN_GROUP
    flat_e = expert.reshape(-1).astype(jnp.int32)
    flat_tok = jnp.repeat(jnp.arange(N, dtype=jnp.int32), TOP_K_IN_GROUP)
    flat_gate = gate.reshape(-1)
    order = jnp.argsort(flat_e)
    e_sorted, tok_sorted, gate_sorted = flat_e[order], flat_tok[order], flat_gate[order]
    counts = jnp.bincount(flat_e, length=N_EXPERTS)
    padded = (counts + EXPERT_BLOCK - 1) // EXPERT_BLOCK * EXPERT_BLOCK
    start = jnp.cumsum(counts) - counts
    pend = jnp.cumsum(padded)
    pstart = pend - padded
    dest = pstart[e_sorted] + (jnp.arange(A, dtype=jnp.int32) - start[e_sorted])
    n_blocks = (A + N_EXPERTS * (EXPERT_BLOCK - 1) + EXPERT_BLOCK - 1) // EXPERT_BLOCK
    rows = jnp.zeros((n_blocks * EXPERT_BLOCK, D), tok.dtype).at[dest].set(tok[tok_sorted])
    blk_e = jnp.minimum(jnp.searchsorted(pend, jnp.arange(n_blocks) * EXPERT_BLOCK, side='right'), N_EXPERTS - 1)

    def expert_block(args):
        xb, e = args
        hb = jax.nn.silu(xb @ w1[e]) * (xb @ w3[e])
        return hb @ w2[e]

    y_rows = lax.map(expert_block, (rows.reshape(n_blocks, EXPERT_BLOCK, D), blk_e)).reshape(-1, D)
    y = y_rows[dest].astype(jnp.float32) * gate_sorted[:, None]
    out = jnp.zeros((N, D), jnp.float32).at[tok_sorted].add(y)
    return out.reshape(B, S, D).astype(h.dtype)


def setup_inputs(seed: int = 0) -> dict:
    key = jax.random.key(seed)
    ks = jax.random.split(key, 24)
    f32 = jnp.float32
    nrm = lambda k, shape, s: jax.random.normal(k, shape, f32) * s
    LAT = Q_LORA_RANK + KV_LORA_RANK + QK_ROPE_DIM
    return {
        'x': jax.random.normal(ks[0], (BATCH, SEQ, D_MODEL), f32),
        'positions': (jnp.arange(SEQ, dtype=jnp.int32)[None, :]
                      + jax.random.randint(ks[1], (BATCH, 1), 0, 4096, dtype=jnp.int32)),
        'ln_g': 1.0 + nrm(ks[2], (DEPTH, 2, D_MODEL), 0.02),
        'ln_b': nrm(ks[3], (DEPTH, 2, D_MODEL), 0.02),
        'mla_w_in': nrm(ks[4], (N_ATTN_LAYERS, D_MODEL, LAT), D_MODEL ** -0.5),
        'mla_q_norm': 1.0 + nrm(ks[5], (N_ATTN_LAYERS, Q_LORA_RANK), 0.02),
        'mla_w_uq': nrm(ks[6], (N_ATTN_LAYERS, Q_LORA_RANK, N_HEADS * (QK_NOPE_DIM + QK_ROPE_DIM)), Q_LORA_RANK ** -0.5),
        'mla_kv_norm': 1.0 + nrm(ks[7], (N_ATTN_LAYERS, KV_LORA_RANK), 0.02),
        'mla_w_ukv': nrm(ks[8], (N_ATTN_LAYERS, KV_LORA_RANK, N_HEADS * (QK_NOPE_DIM + V_HEAD_DIM)), KV_LORA_RANK ** -0.5),
        'mla_w_o': nrm(ks[9], (N_ATTN_LAYERS, N_HEADS * V_HEAD_DIM, D_MODEL), (N_HEADS * V_HEAD_DIM) ** -0.5 * DEEPNORM_BETA),
        'pool_w': nrm(ks[10], (N_POOL_LAYERS, N_POOL_GROUPS, POOL_GROUP_DIM, POOL_GROUP_DIM), POOL_GROUP_DIM ** -0.5 * DEEPNORM_BETA),
        'pool_b': nrm(ks[11], (N_POOL_LAYERS, D_MODEL), 0.02),
        'pool_scale': 1.0 + nrm(ks[12], (N_POOL_LAYERS, D_MODEL), 0.1),
        'moe_w_grp': nrm(ks[13], (DEPTH, D_MODEL, N_EXPERT_GROUPS), D_MODEL ** -0.5),
        'moe_b_grp': nrm(ks[14], (DEPTH, N_EXPERT_GROUPS), 0.01),
        'moe_w_exp': nrm(ks[15], (DEPTH, D_MODEL, N_EXPERTS), D_MODEL ** -0.5),
        'moe_b_exp': nrm(ks[16], (DEPTH, N_EXPERTS), 0.01),
        'moe_w1': nrm(ks[17], (DEPTH, N_EXPERTS, D_MODEL, D_EXPERT), D_MODEL ** -0.5),
        'moe_w3': nrm(ks[18], (DEPTH, N_EXPERTS, D_MODEL, D_EXPERT), D_MODEL ** -0.5),
        'moe_w2': nrm(ks[19], (DEPTH, N_EXPERTS, D_EXPERT, D_MODEL), D_EXPERT ** -0.5 * DEEPNORM_BETA),
    }


def reference(x, positions, ln_g, ln_b, mla_w_in, mla_q_norm, mla_w_uq, mla_kv_norm, mla_w_ukv, mla_w_o,
              pool_w, pool_b, pool_scale, moe_w_grp, moe_b_grp, moe_w_exp, moe_b_exp, moe_w1, moe_w3, moe_w2):
    cos, sin = rope_tables(positions)
    for i in range(DEPTH):
        j = i // N_MIXERS
        if i % N_MIXERS == 0:
            f = mla_mixer(x, cos, sin, mla_w_in[j], mla_q_norm[j], mla_w_uq[j],
                          mla_kv_norm[j], mla_w_ukv[j], mla_w_o[j])
        else:
            f = pool_mixer(x, pool_w[j], pool_b[j], pool_scale[j])
        x = layer_norm(DEEPNORM_ALPHA * x + f, ln_g[i, 0], ln_b[i, 0])
        f = hier_moe(x, moe_w_grp[i], moe_b_grp[i], moe_w_exp[i], moe_b_exp[i],
                     moe_w1[i], moe_w3[i], moe_w2[i])
        x = layer_norm(DEEPNORM_ALPHA * x + f, ln_g[i, 1], ln_b[i, 1])
    return x
```

```python
import functools
import math

import jax
import jax.numpy as jnp
from jax import lax
from jax.experimental import pallas as pl
from jax.experimental.pallas import tpu as pltpu

F32 = jnp.float32
BF16 = jnp.bfloat16
I32 = jnp.int32

N_HEADS = 16
QK_NOPE = 64
QK_ROPE = 32
HALF_ROPE = QK_ROPE // 2
V_DIM = 64
Q_LORA = 256
KV_LORA = 128
ROPE_THETA = 10000.0
POOL_WINDOWS = (2, 4, 8, 16)
N_GROUPS = 8
EXP_PER_GROUP = 8
N_EXPERTS = N_GROUPS * EXP_PER_GROUP
TOP_K = 2
LN_EPS = 1e-5
RMS_EPS = 1e-6

LANES = 128
HEAD_PAD = 128
V_ROWS = 80
POOL_HALO = 16

TOK_TILE = 512
ATT_TQ = 512
ATT_TK = 512
ROUTE_TILE = 512
DISPATCH_TILE = 512
COMBINE_TILE = 256
EXPERT_BLOCK = 256

NEG_BIG = -1e30
_NT = (((1,), (1,)), ((), ()))
_TN = (((0,), (0,)), ((), ()))


def _layer_norm(y, g, b):
    mu = jnp.mean(y, axis=-1, keepdims=True)
    yc = y - mu
    var = jnp.mean(yc * yc, axis=-1, keepdims=True)
    return yc * lax.rsqrt(var + LN_EPS) * g + b


def _mla_proj_kernel(x_ref, cost_ref, sint_ref, cosp_ref, sinp_ref, wfm_ref, wtok_ref, wuq_ref,
                     wv_ref, wkn_ref, qn_ref, kvnc_ref, kvnr_ref, qt_ref, k_ref, vt_ref, *, q_scale):
    t = x_ref.shape[0]
    xb = x_ref[...].astype(BF16)
    lat_t = lax.dot_general(wfm_ref[...], xb, _NT, preferred_element_type=F32)
    cq = lat_t[0:Q_LORA]
    cq = cq * lax.rsqrt(jnp.mean(cq * cq, axis=0, keepdims=True) + RMS_EPS) * qn_ref[...]
    ckv = lat_t[Q_LORA:Q_LORA + KV_LORA]
    ckv = ckv * lax.rsqrt(jnp.mean(ckv * ckv, axis=0, keepdims=True) + RMS_EPS) * kvnc_ref[...]

    q_t = jnp.dot(wuq_ref[...], cq.astype(BF16), preferred_element_type=F32) * q_scale
    n_nope = N_HEADS * QK_NOPE
    n_half = N_HEADS * HALF_ROPE
    qn = q_t[0:n_nope].reshape(N_HEADS, QK_NOPE, t)
    x1 = q_t[n_nope:n_nope + n_half].reshape(N_HEADS, HALF_ROPE, t)
    x2 = q_t[n_nope + n_half:n_nope + 2 * n_half].reshape(N_HEADS, HALF_ROPE, t)
    c = cost_ref[...]
    s = sint_ref[...]
    qt_ref[0, :, 0:QK_NOPE, :] = qn.astype(BF16)
    qt_ref[0, :, QK_NOPE:QK_NOPE + HALF_ROPE, :] = (x1 * c - x2 * s).astype(BF16)
    qt_ref[0, :, QK_NOPE + HALF_ROPE:QK_NOPE + QK_ROPE, :] = (x2 * c + x1 * s).astype(BF16)
    qt_ref[0, :, QK_NOPE + QK_ROPE:HEAD_PAD, :] = jnp.zeros(
        (N_HEADS, HEAD_PAD - QK_NOPE - QK_ROPE, t), BF16)

    v_t = jnp.dot(wv_ref[...], ckv.astype(BF16), preferred_element_type=F32)
    vt_ref[0, :, 0:V_DIM, :] = v_t.reshape(N_HEADS, V_DIM, t).astype(BF16)
    vt_ref[0, :, V_DIM:V_ROWS, :] = jnp.ones((N_HEADS, V_ROWS - V_DIM, t), BF16)

    lat = jnp.dot(xb, wtok_ref[...], preferred_element_type=F32)
    ckv_tok = lat[:, 0:KV_LORA]
    ckv_tok = ckv_tok * lax.rsqrt(jnp.mean(ckv_tok * ckv_tok, axis=-1, keepdims=True) + RMS_EPS) * kvnr_ref[...]
    k_rope = lat[:, KV_LORA:KV_LORA + LANES] * cosp_ref[...] + lat[:, KV_LORA + LANES:] * sinp_ref[...]
    kn = jnp.dot(ckv_tok.astype(BF16), wkn_ref[...], preferred_element_type=F32)
    for h in range(N_HEADS):
        k_ref[0, h] = (kn[:, h * HEAD_PAD:(h + 1) * HEAD_PAD] + k_rope).astype(BF16)


def _mla_proj(x2d, tabs, w, batch, seq):
    n, d = x2d.shape
    t = TOK_TILE
    tiles_per_seq = seq // t
    q_scale = (QK_NOPE + QK_ROPE) ** -0.5 * math.log2(math.e)
    full = lambda a: pl.BlockSpec(a.shape, lambda i: (0,) * a.ndim)
    bs_idx = lambda i: (i // tiles_per_seq, 0, 0, i % tiles_per_seq)
    return pl.pallas_call(
        functools.partial(_mla_proj_kernel, q_scale=q_scale),
        grid=(n // t,),
        in_specs=[
            pl.BlockSpec((t, d), lambda i: (i, 0)),
            pl.BlockSpec((1, HALF_ROPE, t), lambda i: (i // tiles_per_seq, 0, i % tiles_per_seq)),
            pl.BlockSpec((1, HALF_ROPE, t), lambda i: (i // tiles_per_seq, 0, i % tiles_per_seq)),
            pl.BlockSpec((t, LANES), lambda i: (i, 0)),
            pl.BlockSpec((t, LANES), lambda i: (i, 0)),
            full(w['w_fm']), full(w['w_tok']), full(w['w_uq_t']), full(w['w_v_t']), full(w['w_kn']),
            full(w['qn_col']), full(w['kvn_col']), full(w['kvn_row']),
        ],
        out_specs=[
            pl.BlockSpec((1, N_HEADS, HEAD_PAD, t), bs_idx),
            pl.BlockSpec((1, N_HEADS, t, HEAD_PAD), lambda i: (i // tiles_per_seq, 0, i % tiles_per_seq, 0)),
            pl.BlockSpec((1, N_HEADS, V_ROWS, t), bs_idx),
        ],
        out_shape=[
            jax.ShapeDtypeStruct((batch, N_HEADS, HEAD_PAD, seq), BF16),
            jax.ShapeDtypeStruct((batch, N_HEADS, seq, HEAD_PAD), BF16),
            jax.ShapeDtypeStruct((batch, N_HEADS, V_ROWS, seq), BF16),
        ],
        compiler_params=pltpu.CompilerParams(dimension_semantics=("arbitrary",),
                                             vmem_limit_bytes=56 * 1024 * 1024),
        name="mla_proj",
    )(x2d, tabs['cos_t'], tabs['sin_t'], tabs['cos_pad'], tabs['sin_pad'],
      w['w_fm'], w['w_tok'], w['w_uq_t'], w['w_v_t'], w['w_kn'], w['qn_col'], w['kvn_col'], w['kvn_row'])


def _attn_kernel(qt_ref, k_ref, vt_ref, o_ref):
    seq = k_ref.shape[2]
    tq, tk = ATT_TQ, ATT_TK
    n_q = seq // tq

    def k_step(j, carry, q_t, masked):
        m, acc = carry
        k0 = pl.multiple_of(j * tk, tk)
        kb = k_ref[0, 0, pl.ds(k0, tk), :]
        s = jnp.dot(kb, q_t, preferred_element_type=F32)
        if masked:
            row = lax.broadcasted_iota(I32, (tk, tq), 0)
            col = lax.broadcasted_iota(I32, (tk, tq), 1)
            s = jnp.where(row <= col, s, NEG_BIG)
        m_new = jnp.maximum(m, jnp.max(s, axis=0, keepdims=True))
        p = jnp.exp2(s - m_new)
        alpha = jnp.exp2(m - m_new)
        vb = vt_ref[0, 0, :, pl.ds(k0, tk)]
        acc = alpha * acc + jnp.dot(vb, p.astype(BF16), preferred_element_type=F32)
        return m_new, acc

    def q_tile(qi, carry):
        q0 = pl.multiple_of(qi * tq, tq)
        q_t = qt_ref[0, 0, :, pl.ds(q0, tq)]
        init = (jnp.full((1, tq), NEG_BIG, F32), jnp.zeros((V_ROWS, tq), F32))
        m, acc = lax.fori_loop(0, qi, lambda j, c: k_step(j, c, q_t, False), init)
        m, acc = k_step(qi, (m, acc), q_t, True)
        o_ref[0, :, pl.ds(q0, tq)] = (acc[0:V_DIM] / acc[V_DIM:V_DIM + 1]).astype(BF16)
        return carry

    lax.fori_loop(0, n_q, q_tile, 0)


def _attention(q_t, k, v_t):
    batch, heads, _, seq = q_t.shape
    return pl.pallas_call(
        _attn_kernel,
        grid=(batch, heads),
        in_specs=[
            pl.BlockSpec((1, 1, HEAD_PAD, seq), lambda b, h: (b, h, 0, 0)),
            pl.BlockSpec((1, 1, seq, HEAD_PAD), lambda b, h: (b, h, 0, 0)),
            pl.BlockSpec((1, 1, V_ROWS, seq), lambda b, h: (b, h, 0, 0)),
        ],
        out_specs=pl.BlockSpec((1, V_DIM, seq), lambda b, h: (b, h, 0)),
        out_shape=jax.ShapeDtypeStruct((batch, heads * V_DIM, seq), BF16),
        compiler_params=pltpu.CompilerParams(dimension_semantics=("arbitrary", "arbitrary"),
                                             vmem_limit_bytes=56 * 1024 * 1024),
        name="mla_attention",
    )(q_t, k, v_t)


def _outproj_ln_kernel(ot_ref, x_ref, wo_ref, g_ref, b_ref, out_ref, *, alpha):
    f = lax.dot_general(ot_ref[0], wo_ref[...], _TN, preferred_element_type=F32)
    out_ref[...] = _layer_norm(alpha * x_ref[...] + f, g_ref[...], b_ref[...])


def _outproj_ln(o_t, x2d, w_o, g, b, alpha):
    n, d = x2d.shape
    batch, hd, seq = o_t.shape
    t = TOK_TILE
    tiles_per_seq = seq // t
    return pl.pallas_call(
        functools.partial(_outproj_ln_kernel, alpha=alpha),
        grid=(n // t,),
        in_specs=[
            pl.BlockSpec((1, hd, t), lambda i: (i // tiles_per_seq, 0, i % tiles_per_seq)),
            pl.BlockSpec((t, d), lambda i: (i, 0)),
            pl.BlockSpec((hd, d), lambda i: (0, 0)),
            pl.BlockSpec((1, d), lambda i: (0, 0)),
            pl.BlockSpec((1, d), lambda i: (0, 0)),
        ],
        out_specs=pl.BlockSpec((t, d), lambda i: (i, 0)),
        out_shape=jax.ShapeDtypeStruct((n, d), F32),
        compiler_params=pltpu.CompilerParams(dimension_semantics=("arbitrary",)),
        name="mla_outproj_ln",
    )(o_t, x2d, w_o, g, b)


def _pool_ln_kernel(x_ref, halo_ref, w_ref, pb_ref, ps_ref, g_ref, b_ref, out_ref, *, alpha, tiles_per_seq):
    t, d = x_ref.shape
    gd = d // len(POOL_WINDOWS)
    i = pl.program_id(0)
    tile_in_seq = i % tiles_per_seq
    x = x_ref[...]
    halo = jnp.where(tile_in_seq == 0, 0.0, halo_ref[...])
    ext = jnp.concatenate([halo, x], axis=0)
    pos = tile_in_seq * t + lax.broadcasted_iota(I32, (t, 1), 0)
    ys = []
    for gi, win in enumerate(POOL_WINDOWS):
        e = ext[:, gi * gd:(gi + 1) * gd]
        sh = 1
        while sh < win:
            e = e + pltpu.roll(e, sh, axis=0)
            sh *= 2
        cnt = jnp.minimum(pos + 1, win).astype(F32)
        pooled = e[POOL_HALO:] / cnt - x[:, gi * gd:(gi + 1) * gd]
        ys.append(jnp.dot(pooled.astype(BF16), w_ref[gi], preferred_element_type=F32))
    y = (jnp.concatenate(ys, axis=1) + pb_ref[...]) * ps_ref[...]
    out_ref[...] = _layer_norm(alpha * x + y, g_ref[...], b_ref[...])


def _pool_ln(x2d, seq, w, pb, ps, g, b, alpha):
    n, d = x2d.shape
    t = TOK_TILE
    tiles_per_seq = seq // t
    halo_blocks = t // POOL_HALO
    vec = pl.BlockSpec((1, d), lambda i: (0, 0))
    return pl.pallas_call(
        functools.partial(_pool_ln_kernel, alpha=alpha, tiles_per_seq=tiles_per_seq),
        grid=(n // t,),
        in_specs=[
            pl.BlockSpec((t, d), lambda i: (i, 0)),
            pl.BlockSpec((POOL_HALO, d), lambda i: (jnp.maximum(i * halo_blocks - 1, 0), 0)),
            pl.BlockSpec(w.shape, lambda i: (0, 0, 0)),
            vec, vec, vec, vec,
        ],
        out_specs=pl.BlockSpec((t, d), lambda i: (i, 0)),
        out_shape=jax.ShapeDtypeStruct((n, d), F32),
        compiler_params=pltpu.CompilerParams(dimension_semantics=("arbitrary",)),
        name="pool_ln",
    )(x2d, x2d, w, pb, ps, g, b)


def _router_kernel(x_ref, whi_ref, wlo_ref, bias_ref, tri_ref, eid_ref, gate_ref, rank_ref, cnt_ref, run_ref):
    t = x_ref.shape[0]

    @pl.when(pl.program_id(0) == 0)
    def _():
        run_ref[...] = jnp.zeros_like(run_ref)

    x = x_ref[...]
    xh = x.astype(BF16)
    xl = (x - xh.astype(F32)).astype(BF16)
    lg = (lax.dot_general(whi_ref[...], xh, _NT, preferred_element_type=F32)
          + lax.dot_general(whi_ref[...], xl, _NT, preferred_element_type=F32)
          + lax.dot_general(wlo_ref[...], xh, _NT, preferred_element_type=F32)
          + bias_ref[...])
    ridx = lax.broadcasted_iota(I32, (N_GROUPS, t), 0)

    g = lg[0:N_GROUPS]
    gmax = jnp.max(g, axis=0, keepdims=True)
    g_sel = jnp.min(jnp.where(g == gmax, ridx, N_GROUPS), axis=0, keepdims=True)
    g_gate = 1.0 / jnp.sum(jnp.exp(g - gmax), axis=0, keepdims=True)

    e_all = lg[N_GROUPS:N_GROUPS + N_EXPERTS]
    e = jnp.zeros((EXP_PER_GROUP, t), F32)
    for gi in range(N_GROUPS):
        e = e + jnp.where(g_sel == gi, e_all[gi * EXP_PER_GROUP:(gi + 1) * EXP_PER_GROUP], 0.0)
    pe = jnp.exp(e - jnp.max(e, axis=0, keepdims=True))
    p1 = jnp.max(pe, axis=0, keepdims=True)
    i1 = jnp.min(jnp.where(pe == p1, ridx, EXP_PER_GROUP), axis=0, keepdims=True)
    pe2 = jnp.where(ridx == i1, -1.0, pe)
    p2 = jnp.max(pe2, axis=0, keepdims=True)
    i2 = jnp.min(jnp.where(pe2 == p2, ridx, EXP_PER_GROUP), axis=0, keepdims=True)
    denom = p1 + p2
    e1 = g_sel * EXP_PER_GROUP + i1
    e2 = g_sel * EXP_PER_GROUP + i2
    eid_ref[0:1, :] = e1
    eid_ref[1:2, :] = e2
    gate_ref[0:1, :] = g_gate * p1 / denom
    gate_ref[1:2, :] = g_gate * p2 / denom

    eidx = lax.broadcasted_iota(I32, (N_EXPERTS, t), 0)
    oh1 = eidx == e1
    oh2 = eidx == e2
    oh = jnp.where(oh1 | oh2, 1.0, 0.0)
    earlier = jnp.dot(oh.astype(BF16), tri_ref[...], preferred_element_type=F32)
    tot = earlier + run_ref[:, 0:1]
    rank_ref[0:1, :] = jnp.sum(jnp.where(oh1, tot, 0.0), axis=0, keepdims=True).astype(I32)
    rank_ref[1:2, :] = jnp.sum(jnp.where(oh2, tot, 0.0), axis=0, keepdims=True).astype(I32)
    run_ref[...] = run_ref[...] + jnp.sum(oh, axis=1, keepdims=True)
    cnt_ref[...] = run_ref[...].astype(I32)


def _router(x2d, w_hi, w_lo, bias, tri):
    n, d = x2d.shape
    t = ROUTE_TILE
    tok = lambda dt: jax.ShapeDtypeStruct((TOP_K, n), dt)
    tok_spec = pl.BlockSpec((TOP_K, t), lambda i: (0, i))
    return pl.pallas_call(
        _router_kernel,
        grid=(n // t,),
        in_specs=[
            pl.BlockSpec((t, d), lambda i: (i, 0)),
            pl.BlockSpec(w_hi.shape, lambda i: (0, 0)),
            pl.BlockSpec(w_lo.shape, lambda i: (0, 0)),
            pl.BlockSpec(bias.shape, lambda i: (0, 0)),
            pl.BlockSpec(tri.shape, lambda i: (0, 0)),
        ],
        out_specs=[tok_spec, tok_spec, tok_spec, pl.BlockSpec((N_EXPERTS, LANES), lambda i: (0, 0))],
        out_shape=[tok(I32), tok(F32), tok(I32), jax.ShapeDtypeStruct((N_EXPERTS, LANES), I32)],
        scratch_shapes=[pltpu.VMEM((N_EXPERTS, LANES), F32)],
        compiler_params=pltpu.CompilerParams(dimension_semantics=("arbitrary",)),
        name="moe_router",
    )(x2d, w_hi, w_lo, bias, tri)


def _row_copy(src, src_row, dst, dst_row, sem):
    return pltpu.make_async_copy(src.at[pl.ds(src_row, 1)], dst.at[pl.ds(dst_row, 1)], sem)


def _dispatch_kernel(dest_ref, x_hbm, rows_in_hbm, rows_hbm, sem):
    del rows_in_hbm
    t = dest_ref.shape[1]
    i = pl.program_id(0)
    t0 = i * t

    def issue(r, c):
        for k in range(TOP_K):
            _row_copy(x_hbm, t0 + r, rows_hbm, dest_ref[k, r], sem).start()
        return c

    def drain(r, c):
        for k in range(TOP_K):
            _row_copy(x_hbm, 0, rows_hbm, 0, sem).wait()
        return c

    lax.fori_loop(0, t, issue, 0, unroll=8)

    @pl.when(i > 0)
    def _():
        lax.fori_loop(0, t, drain, 0, unroll=8)

    @pl.when(i == pl.num_programs(0) - 1)
    def _():
        lax.fori_loop(0, t, drain, 0, unroll=8)


def _dispatch(dest, x2d, rows_init):
    n, d = x2d.shape
    t = DISPATCH_TILE
    return pl.pallas_call(
        _dispatch_kernel,
        grid=(n // t,),
        in_specs=[
            pl.BlockSpec((TOP_K, t), lambda i: (0, i), memory_space=pltpu.SMEM),
            pl.BlockSpec(memory_space=pl.ANY),
            pl.BlockSpec(memory_space=pl.ANY),
        ],
        out_specs=pl.BlockSpec(memory_space=pl.ANY),
        out_shape=jax.ShapeDtypeStruct(rows_init.shape, rows_init.dtype),
        scratch_shapes=[pltpu.SemaphoreType.DMA(())],
        input_output_aliases={2: 0},
        compiler_params=pltpu.CompilerParams(dimension_semantics=("arbitrary",), has_side_effects=True),
        name="moe_dispatch",
    )(dest, x2d, rows_init)


def _ffn_kernel(be_ref, nu_ref, rows_ref, w1_ref, w3_ref, w2_ref, y_ref, w1b, w3b, w2b):
    b = pl.program_id(0)

    @pl.when(b < nu_ref[0])
    def _():
        prev = be_ref[jnp.maximum(b - 1, 0)]

        @pl.when((b == 0) | (be_ref[b] != prev))
        def _():
            w1b[...] = w1_ref[0].astype(BF16)
            w3b[...] = w3_ref[0].astype(BF16)
            w2b[...] = w2_ref[0].astype(BF16)

        xb = rows_ref[...].astype(BF16)
        h1 = jnp.dot(xb, w1b[...], preferred_element_type=F32)
        h3 = jnp.dot(xb, w3b[...], preferred_element_type=F32)
        hb = h1 * jax.nn.sigmoid(h1) * h3
        y_ref[...] = jnp.dot(hb.astype(BF16), w2b[...], preferred_element_type=F32)


def _expert_ffn(blk_e, n_used, rows, w1, w3, w2):
    r, d = rows.shape
    de = w1.shape[2]
    bm = EXPERT_BLOCK
    nb = r // bm
    clamp = lambda b, nu: jnp.minimum(b, nu[0] - 1)
    row_spec = pl.BlockSpec((bm, d), lambda b, be, nu: (clamp(b, nu), 0))
    return pl.pallas_call(
        _ffn_kernel,
        grid_spec=pltpu.PrefetchScalarGridSpec(
            num_scalar_prefetch=2,
            grid=(nb,),
            in_specs=[
                row_spec,
                pl.BlockSpec((1, d, de), lambda b, be, nu: (be[clamp(b, nu)], 0, 0)),
                pl.BlockSpec((1, d, de), lambda b, be, nu: (be[clamp(b, nu)], 0, 0)),
                pl.BlockSpec((1, de, d), lambda b, be, nu: (be[clamp(b, nu)], 0, 0)),
            ],
            out_specs=row_spec,
            scratch_shapes=[pltpu.VMEM((d, de), BF16), pltpu.VMEM((d, de), BF16), pltpu.VMEM((de, d), BF16)],
        ),
        out_shape=jax.ShapeDtypeStruct((r, d), F32),
        compiler_params=pltpu.CompilerParams(dimension_semantics=("arbitrary",)),
        name="moe_expert_ffn",
    )(blk_e, n_used, rows, w1, w3, w2)


def _combine_ln_kernel(dest_ref, gate_ref, x_ref, y_hbm, g_ref, b_ref, out_ref, ybuf, sem, *, alpha):
    t = x_ref.shape[0]

    def issue(r, c):
        for k in range(TOP_K):
            _row_copy(y_hbm, dest_ref[k, r], ybuf.at[k], r, sem).start()
        return c

    def drain(r, c):
        for k in range(TOP_K):
            _row_copy(y_hbm, 0, ybuf.at[k], 0, sem).wait()
        return c

    lax.fori_loop(0, t, issue, 0, unroll=8)
    lax.fori_loop(0, t, drain, 0, unroll=8)
    gate = gate_ref[...]
    y = alpha * x_ref[...] + gate[:, 0:1] * ybuf[0] + gate[:, 1:2] * ybuf[1]
    out_ref[...] = _layer_norm(y, g_ref[...], b_ref[...])


def _combine_ln(dest, gates_tok, x2d, y_rows, g, b, alpha):
    n, d = x2d.shape
    t = COMBINE_TILE
    vec = pl.BlockSpec((1, d), lambda i: (0, 0))
    return pl.pallas_call(
        functools.partial(_combine_ln_kernel, alpha=alpha),
        grid=(n // t,),
        in_specs=[
            pl.BlockSpec((TOP_K, t), lambda i: (0, i), memory_space=pltpu.SMEM),
            pl.BlockSpec((t, TOP_K), lambda i: (i, 0)),
            pl.BlockSpec((t, d), lambda i: (i, 0)),
            pl.BlockSpec(memory_space=pl.ANY),
            vec, vec,
        ],
        out_specs=pl.BlockSpec((t, d), lambda i: (i, 0)),
        out_shape=jax.ShapeDtypeStruct((n, d), F32),
        scratch_shapes=[pltpu.VMEM((TOP_K, t, d), F32), pltpu.SemaphoreType.DMA(())],
        compiler_params=pltpu.CompilerParams(dimension_semantics=("arbitrary",)),
        name="moe_combine_ln",
    )(dest, gates_tok, x2d, y_rows, g, b)


def _rope_tables(positions):
    inv_freq = ROPE_THETA ** (-jnp.arange(0, QK_ROPE, 2, dtype=F32) / QK_ROPE)
    ang = positions.astype(F32)[..., None] * inv_freq
    cos, sin = jnp.cos(ang), jnp.sin(ang)
    b, s, _ = cos.shape
    zeros = jnp.zeros((b * s, QK_NOPE), F32)
    tail = jnp.zeros((b * s, LANES - QK_NOPE - QK_ROPE), F32)
    c2, s2 = cos.reshape(b * s, -1), sin.reshape(b * s, -1)
    return {
        'cos_t': jnp.swapaxes(cos, 1, 2), 'sin_t': jnp.swapaxes(sin, 1, 2),
        'cos_pad': jnp.concatenate([zeros, c2, c2, tail], axis=1),
        'sin_pad': jnp.concatenate([zeros, s2, s2, tail], axis=1),
    }


def _mla_weights(w_in, q_norm, w_uq, kv_norm, w_ukv, w_o):
    d = w_in.shape[0]
    lat_q_kv = Q_LORA + KV_LORA
    r1 = w_in[:, lat_q_kv:lat_q_kv + HALF_ROPE]
    r2 = w_in[:, lat_q_kv + HALF_ROPE:lat_q_kv + QK_ROPE]
    z_lo = jnp.zeros((d, QK_NOPE), F32)
    z_hi = jnp.zeros((d, LANES - QK_NOPE - QK_ROPE), F32)
    w_tok = jnp.concatenate([w_in[:, Q_LORA:lat_q_kv],
                             z_lo, r1, r2, z_hi,
                             z_lo, -r2, r1, z_hi],
                            axis=1)
    uq = w_uq.reshape(Q_LORA, N_HEADS, QK_NOPE + QK_ROPE)
    w_uq_t = jnp.concatenate([uq[:, :, :QK_NOPE].reshape(Q_LORA, -1),
                              uq[:, :, QK_NOPE:QK_NOPE + HALF_ROPE].reshape(Q_LORA, -1),
                              uq[:, :, QK_NOPE + HALF_ROPE:].reshape(Q_LORA, -1)], axis=1).T
    ukv = w_ukv.reshape(KV_LORA, N_HEADS, QK_NOPE + V_DIM)
    w_kn = jnp.concatenate([ukv[:, :, :QK_NOPE], jnp.zeros((KV_LORA, N_HEADS, HEAD_PAD - QK_NOPE), F32)],
                           axis=2).reshape(KV_LORA, N_HEADS * HEAD_PAD)
    w_v_t = ukv[:, :, QK_NOPE:].reshape(KV_LORA, N_HEADS * V_DIM).T
    return {
        'w_fm': w_in[:, :lat_q_kv].T.astype(BF16), 'w_tok': w_tok.astype(BF16),
        'w_uq_t': w_uq_t.astype(BF16), 'w_v_t': w_v_t.astype(BF16), 'w_kn': w_kn.astype(BF16),
        'qn_col': q_norm.reshape(-1, 1), 'kvn_col': kv_norm.reshape(-1, 1), 'kvn_row': kv_norm.reshape(1, -1),
        'w_o': w_o.astype(BF16),
    }


def _router_weights(w_grp, b_grp, w_exp, b_exp):
    d = w_grp.shape[0]
    pad = LANES - N_GROUPS - N_EXPERTS
    w = jnp.concatenate([w_grp, w_exp, jnp.zeros((d, pad), F32)], axis=1).T
    w_hi = w.astype(BF16)
    w_lo = (w - w_hi.astype(F32)).astype(BF16)
    bias = jnp.concatenate([b_grp, b_exp, jnp.zeros((pad,), F32)]).reshape(-1, 1)
    return w_hi, w_lo, bias


def _moe_ln(x2d, w_hi, w_lo, bias, tri, w1, w3, w2, g, b, alpha):
    n, d = x2d.shape
    bm = EXPERT_BLOCK
    n_assign = n * TOP_K
    nb = -(-(n_assign + N_EXPERTS * (bm - 1)) // bm)
    eid, gates, rank, cnt = _router(x2d, w_hi, w_lo, bias, tri)
    counts = cnt[:, 0]
    padded = (counts + bm - 1) // bm * bm
    pend = jnp.cumsum(padded)
    pstart = pend - padded
    dest = pstart[eid] + rank
    n_used = (pend[-1:] // bm).astype(I32)
    blk_e = jnp.minimum(jnp.searchsorted(pend, jnp.arange(nb, dtype=I32) * bm, side='right'),
                        N_EXPERTS - 1).astype(I32)
    rows = _dispatch(dest, x2d, jnp.zeros((nb * bm, d), F32))
    y_rows = _expert_ffn(blk_e, n_used, rows, w1, w3, w2)
    return _combine_ln(dest, gates.T, x2d, y_rows, g, b, alpha)


def kernel(x, positions, ln_g, ln_b, mla_w_in, mla_q_norm, mla_w_uq, mla_kv_norm, mla_w_ukv, mla_w_o,
           pool_w, pool_b, pool_scale, moe_w_grp, moe_b_grp, moe_w_exp, moe_b_exp, moe_w1, moe_w3, moe_w2):
    batch, seq, d = x.shape
    depth = ln_g.shape[0]
    alpha = (2 * depth) ** 0.25
    tabs = _rope_tables(positions)
    t = ROUTE_TILE
    tri = (lax.broadcasted_iota(I32, (t, t), 0) < lax.broadcasted_iota(I32, (t, t), 1)).astype(BF16)
    x2d = x.reshape(batch * seq, d)
    vec = lambda v: v.reshape(1, d)
    for i in range(depth):
        j = i // 2
        if i % 2 == 0:
            w = _mla_weights(mla_w_in[j], mla_q_norm[j], mla_w_uq[j], mla_kv_norm[j], mla_w_ukv[j], mla_w_o[j])
            q_t, k, v_t = _mla_proj(x2d, tabs, w, batch, seq)
            o_t = _attention(q_t, k, v_t)
            x2d = _outproj_ln(o_t, x2d, w['w_o'], vec(ln_g[i, 0]), vec(ln_b[i, 0]), alpha)
        else:
            x2d = _pool_ln(x2d, seq, pool_w[j].astype(BF16), vec(pool_b[j]), vec(pool_scale[j]),
                           vec(ln_g[i, 0]), vec(ln_b[i, 0]), alpha)
        w_hi, w_lo, bias = _router_weights(moe_w_grp[i], moe_b_grp[i], moe_w_exp[i], moe_b_exp[i])
        x2d = _moe_ln(x2d, w_hi, w_lo, bias, tri, moe_w1[i], moe_w3[i], moe_w2[i],
                      vec(ln_g[i, 1]), vec(ln_b[i, 1]), alpha)
    return x2d.reshape(batch, seq, d)
```

```python
import functools
import math

import jax
import jax.numpy as jnp
from jax import lax
from jax.experimental import pallas as pl
from jax.experimental.pallas import tpu as pltpu

F32 = jnp.float32
BF16 = jnp.bfloat16
I32 = jnp.int32

N_HEADS = 16
QK_NOPE = 64
QK_ROPE = 32
HALF_ROPE = QK_ROPE // 2
V_DIM = 64
Q_LORA = 256
KV_LORA = 128
ROPE_THETA = 10000.0
POOL_WINDOWS = (2, 4, 8, 16)
N_GROUPS = 8
EXP_PER_GROUP = 8
N_EXPERTS = N_GROUPS * EXP_PER_GROUP
TOP_K = 2
LN_EPS = 1e-5
RMS_EPS = 1e-6

LANES = 128
HEAD_PAD = 128
V_ROWS = 80
POOL_HALO = 16

TOK_TILE = 512
ATT_TQ = 512
ATT_TK = 256
ATT_HEADS_PER_STEP = 1
ROUTE_TILE = 512
DISPATCH_TILE = 512
COMBINE_TILE = 256
EXPERT_BLOCK = 256

NEG_BIG = -1e30
_NT = (((1,), (1,)), ((), ()))
_TN = (((0,), (0,)), ((), ()))


def _layer_norm(y, g, b):
    mu = jnp.mean(y, axis=-1, keepdims=True)
    yc = y - mu
    var = jnp.mean(yc * yc, axis=-1, keepdims=True)
    return yc * lax.rsqrt(var + LN_EPS) * g + b


def _mla_proj_kernel(x_ref, cost_ref, sint_ref, cosp_ref, sinp_ref, wfm_ref, wtok_ref, wuq_ref,
                     wv_ref, wkn_ref, qn_ref, kvnc_ref, kvnr_ref, qt_ref, k_ref, vt_ref, *, q_scale):
    t = x_ref.shape[0]
    xb = x_ref[...].astype(BF16)
    lat_t = lax.dot_general(wfm_ref[...], xb, _NT, preferred_element_type=F32)
    cq = lat_t[0:Q_LORA]
    cq = cq * lax.rsqrt(jnp.mean(cq * cq, axis=0, keepdims=True) + RMS_EPS) * qn_ref[...]
    ckv = lat_t[Q_LORA:Q_LORA + KV_LORA]
    ckv = ckv * lax.rsqrt(jnp.mean(ckv * ckv, axis=0, keepdims=True) + RMS_EPS) * kvnc_ref[...]

    q_t = jnp.dot(wuq_ref[...], cq.astype(BF16), preferred_element_type=F32) * q_scale
    n_nope = N_HEADS * QK_NOPE
    n_half = N_HEADS * HALF_ROPE
    qn = q_t[0:n_nope].reshape(N_HEADS, QK_NOPE, t)
    x1 = q_t[n_nope:n_nope + n_half].reshape(N_HEADS, HALF_ROPE, t)
    x2 = q_t[n_nope + n_half:n_nope + 2 * n_half].reshape(N_HEADS, HALF_ROPE, t)
    c = cost_ref[...]
    s = sint_ref[...]
    qt_ref[0, :, 0:QK_NOPE, :] = qn.astype(BF16)
    qt_ref[0, :, QK_NOPE:QK_NOPE + HALF_ROPE, :] = (x1 * c - x2 * s).astype(BF16)
    qt_ref[0, :, QK_NOPE + HALF_ROPE:QK_NOPE + QK_ROPE, :] = (x2 * c + x1 * s).astype(BF16)
    qt_ref[0, :, QK_NOPE + QK_ROPE:HEAD_PAD, :] = jnp.zeros(
        (N_HEADS, HEAD_PAD - QK_NOPE - QK_ROPE, t), BF16)

    v_t = jnp.dot(wv_ref[...], ckv.astype(BF16), preferred_element_type=F32)
    vt_ref[0, :, 0:V_DIM, :] = v_t.reshape(N_HEADS, V_DIM, t).astype(BF16)
    vt_ref[0, :, V_DIM:V_ROWS, :] = jnp.ones((N_HEADS, V_ROWS - V_DIM, t), BF16)

    lat = jnp.dot(xb, wtok_ref[...], preferred_element_type=F32)
    ckv_tok = lat[:, 0:KV_LORA]
    ckv_tok = ckv_tok * lax.rsqrt(jnp.mean(ckv_tok * ckv_tok, axis=-1, keepdims=True) + RMS_EPS) * kvnr_ref[...]
    k_rope = lat[:, KV_LORA:KV_LORA + LANES] * cosp_ref[...] + lat[:, KV_LORA + LANES:] * sinp_ref[...]
    kn = jnp.dot(ckv_tok.astype(BF16), wkn_ref[...], preferred_element_type=F32)
    for h in range(N_HEADS):
        k_ref[0, h] = (kn[:, h * HEAD_PAD:(h + 1) * HEAD_PAD] + k_rope).astype(BF16)


def _mla_proj(x2d, tabs, w, batch, seq):
    n, d = x2d.shape
    t = TOK_TILE
    tiles_per_seq = seq // t
    q_scale = (QK_NOPE + QK_ROPE) ** -0.5 * math.log2(math.e)
    full = lambda a: pl.BlockSpec(a.shape, lambda i: (0,) * a.ndim)
    bs_idx = lambda i: (i // tiles_per_seq, 0, 0, i % tiles_per_seq)
    return pl.pallas_call(
        functools.partial(_mla_proj_kernel, q_scale=q_scale),
        grid=(n // t,),
        in_specs=[
            pl.BlockSpec((t, d), lambda i: (i, 0)),
            pl.BlockSpec((1, HALF_ROPE, t), lambda i: (i // tiles_per_seq, 0, i % tiles_per_seq)),
            pl.BlockSpec((1, HALF_ROPE, t), lambda i: (i // tiles_per_seq, 0, i % tiles_per_seq)),
            pl.BlockSpec((t, LANES), lambda i: (i, 0)),
            pl.BlockSpec((t, LANES), lambda i: (i, 0)),
            full(w['w_fm']), full(w['w_tok']), full(w['w_uq_t']), full(w['w_v_t']), full(w['w_kn']),
            full(w['qn_col']), full(w['kvn_col']), full(w['kvn_row']),
        ],
        out_specs=[
            pl.BlockSpec((1, N_HEADS, HEAD_PAD, t), bs_idx),
            pl.BlockSpec((1, N_HEADS, t, HEAD_PAD), lambda i: (i // tiles_per_seq, 0, i % tiles_per_seq, 0)),
            pl.BlockSpec((1, N_HEADS, V_ROWS, t), bs_idx),
        ],
        out_shape=[
            jax.ShapeDtypeStruct((batch, N_HEADS, HEAD_PAD, seq), BF16),
            jax.ShapeDtypeStruct((batch, N_HEADS, seq, HEAD_PAD), BF16),
            jax.ShapeDtypeStruct((batch, N_HEADS, V_ROWS, seq), BF16),
        ],
        compiler_params=pltpu.CompilerParams(dimension_semantics=("arbitrary",),
                                             vmem_limit_bytes=56 * 1024 * 1024),
        name="mla_proj",
    )(x2d, tabs['cos_t'], tabs['sin_t'], tabs['cos_pad'], tabs['sin_pad'],
      w['w_fm'], w['w_tok'], w['w_uq_t'], w['w_v_t'], w['w_kn'], w['qn_col'], w['kvn_col'], w['kvn_row'])


def _attn_kernel(qt_ref, k_ref, vt_ref, o_ref, s_scr, p_scr):
    n_h = qt_ref.shape[1]
    seq = k_ref.shape[2]
    tq, tk = ATT_TQ, ATT_TK
    n_q = seq // tq
    heads = range(n_h)

    def load_q(q0):
        return [qt_ref[0, h, :, pl.ds(q0, tq)] for h in heads]

    def scores(h, q_t, k0, slot):
        kb = k_ref[0, h, pl.ds(k0, tk), :]
        s = jnp.dot(kb, q_t, preferred_element_type=F32)
        s_scr[h, slot] = s
        return jnp.max(s, axis=0, keepdims=True)

    def softmax(h, slot, m, bmax, diag_off):
        s = s_scr[h, slot]
        if diag_off is not None:
            row = lax.broadcasted_iota(I32, (tk, tq), 0) + diag_off
            col = lax.broadcasted_iota(I32, (tk, tq), 1)
            s = jnp.where(row <= col, s, NEG_BIG)
            bmax = jnp.max(s, axis=0, keepdims=True)
        m_new = jnp.maximum(m, bmax)
        p_scr[h, slot] = jnp.exp2(s - m_new).astype(BF16)
        return m_new, jnp.exp2(m - m_new)

    def values(h, k0, slot, alpha, acc):
        vb = vt_ref[0, h, :, pl.ds(k0, tk)]
        return alpha * acc + jnp.dot(vb, p_scr[h, slot], preferred_element_type=F32)

    def trip(t, carry, q_ts, q_next):
        last = q_next is not None
        m, acc, a_pend, bmax0 = [list(c) for c in carry]
        ka = pl.multiple_of(t * tq, tq)
        k_prev = pl.multiple_of(jnp.maximum(ka - tk, 0), tk)
        a0, bmax1 = [None] * n_h, [None] * n_h
        for h in heads:
            bmax1[h] = scores(h, q_ts[h], ka + tk, 1)
        for h in heads:
            m[h], a0[h] = softmax(h, 0, m[h], bmax0[h], 0 if last else None)
        for h in heads:
            acc[h] = values(h, k_prev, 1, a_pend[h], acc[h])
        for h in heads:
            bmax0[h] = scores(h, q_next[h], 0, 0) if last else scores(h, q_ts[h], ka + tq, 0)
        for h in heads:
            m[h], a_pend[h] = softmax(h, 1, m[h], bmax1[h], tk if last else None)
        for h in heads:
            acc[h] = values(h, ka, 0, a0[h], acc[h])
        return tuple(m), tuple(acc), tuple(a_pend), tuple(bmax0)

    def q_tile(qi, bmax0):
        q0 = pl.multiple_of(qi * tq, tq)
        q_ts = load_q(q0)
        q_next = load_q(pl.multiple_of(jnp.minimum(q0 + tq, seq - tq), tq))
        for h in heads:
            p_scr[h, 1] = jnp.zeros((tk, tq), BF16)
        init = (tuple(jnp.full((1, tq), NEG_BIG, F32) for _ in heads),
                tuple(jnp.zeros((V_ROWS, tq), F32) for _ in heads),
                tuple(jnp.ones((1, tq), F32) for _ in heads),
                bmax0)
        res = lax.fori_loop(0, qi, lambda t, c: trip(t, c, q_ts, None), init)
        m, acc, a_pend, bmax0 = trip(qi, res, q_ts, q_next)
        for h in heads:
            out = values(h, q0 + tk, 1, a_pend[h], acc[h])
            o_ref[0, h * V_DIM:(h + 1) * V_DIM, pl.ds(q0, tq)] = (
                out[0:V_DIM] / out[V_DIM:V_DIM + 1]).astype(BF16)
        return bmax0

    q_first = load_q(0)
    lax.fori_loop(0, n_q, q_tile, tuple(scores(h, q_first[h], 0, 0) for h in heads))


def _attention(q_t, k, v_t):
    batch, heads, _, seq = q_t.shape
    g = ATT_HEADS_PER_STEP
    assert ATT_TQ == 2 * ATT_TK and seq % ATT_TQ == 0 and heads % g == 0
    return pl.pallas_call(
        _attn_kernel,
        grid=(batch, heads // g),
        in_specs=[
            pl.BlockSpec((1, g, HEAD_PAD, seq), lambda b, h: (b, h, 0, 0)),
            pl.BlockSpec((1, g, seq, HEAD_PAD), lambda b, h: (b, h, 0, 0)),
            pl.BlockSpec((1, g, V_ROWS, seq), lambda b, h: (b, h, 0, 0)),
        ],
        out_specs=pl.BlockSpec((1, g * V_DIM, seq), lambda b, h: (b, h, 0)),
        out_shape=jax.ShapeDtypeStruct((batch, heads * V_DIM, seq), BF16),
        scratch_shapes=[pltpu.VMEM((g, 2, ATT_TK, ATT_TQ), F32), pltpu.VMEM((g, 2, ATT_TK, ATT_TQ), BF16)],
        compiler_params=pltpu.CompilerParams(dimension_semantics=("arbitrary", "arbitrary"),
                                             vmem_limit_bytes=56 * 1024 * 1024),
        name="mla_attention",
    )(q_t, k, v_t)


def _outproj_ln_kernel(ot_ref, x_ref, wo_ref, g_ref, b_ref, out_ref, *, alpha):
    f = lax.dot_general(ot_ref[0], wo_ref[...], _TN, preferred_element_type=F32)
    out_ref[...] = _layer_norm(alpha * x_ref[...] + f, g_ref[...], b_ref[...])


def _outproj_ln(o_t, x2d, w_o, g, b, alpha):
    n, d = x2d.shape
    batch, hd, seq = o_t.shape
    t = TOK_TILE
    tiles_per_seq = seq // t
    return pl.pallas_call(
        functools.partial(_outproj_ln_kernel, alpha=alpha),
        grid=(n // t,),
        in_specs=[
            pl.BlockSpec((1, hd, t), lambda i: (i // tiles_per_seq, 0, i % tiles_per_seq)),
            pl.BlockSpec((t, d), lambda i: (i, 0)),
            pl.BlockSpec((hd, d), lambda i: (0, 0)),
            pl.BlockSpec((1, d), lambda i: (0, 0)),
            pl.BlockSpec((1, d), lambda i: (0, 0)),
        ],
        out_specs=pl.BlockSpec((t, d), lambda i: (i, 0)),
        out_shape=jax.ShapeDtypeStruct((n, d), F32),
        compiler_params=pltpu.CompilerParams(dimension_semantics=("arbitrary",)),
        name="mla_outproj_ln",
    )(o_t, x2d, w_o, g, b)


def _pool_ln_kernel(x_ref, halo_ref, w_ref, pb_ref, ps_ref, g_ref, b_ref, out_ref, *, alpha, tiles_per_seq):
    t, d = x_ref.shape
    gd = d // len(POOL_WINDOWS)
    i = pl.program_id(0)
    tile_in_seq = i % tiles_per_seq
    x = x_ref[...]
    halo = jnp.where(tile_in_seq == 0, 0.0, halo_ref[...])
    ext = jnp.concatenate([halo, x], axis=0)
    pos = tile_in_seq * t + lax.broadcasted_iota(I32, (t, 1), 0)
    ys = []
    for gi, win in enumerate(POOL_WINDOWS):
        e = ext[:, gi * gd:(gi + 1) * gd]
        sh = 1
        while sh < win:
            e = e + pltpu.roll(e, sh, axis=0)
            sh *= 2
        cnt = jnp.minimum(pos + 1, win).astype(F32)
        pooled = e[POOL_HALO:] / cnt - x[:, gi * gd:(gi + 1) * gd]
        ys.append(jnp.dot(pooled.astype(BF16), w_ref[gi], preferred_element_type=F32))
    y = (jnp.concatenate(ys, axis=1) + pb_ref[...]) * ps_ref[...]
    out_ref[...] = _layer_norm(alpha * x + y, g_ref[...], b_ref[...])


def _pool_ln(x2d, seq, w, pb, ps, g, b, alpha):
    n, d = x2d.shape
    t = TOK_TILE
    tiles_per_seq = seq // t
    halo_blocks = t // POOL_HALO
    vec = pl.BlockSpec((1, d), lambda i: (0, 0))
    return pl.pallas_call(
        functools.partial(_pool_ln_kernel, alpha=alpha, tiles_per_seq=tiles_per_seq),
        grid=(n // t,),
        in_specs=[
            pl.BlockSpec((t, d), lambda i: (i, 0)),
            pl.BlockSpec((POOL_HALO, d), lambda i: (jnp.maximum(i * halo_blocks - 1, 0), 0)),
            pl.BlockSpec(w.shape, lambda i: (0, 0, 0)),
            vec, vec, vec, vec,
        ],
        out_specs=pl.BlockSpec((t, d), lambda i: (i, 0)),
        out_shape=jax.ShapeDtypeStruct((n, d), F32),
        compiler_params=pltpu.CompilerParams(dimension_semantics=("arbitrary",)),
        name="pool_ln",
    )(x2d, x2d, w, pb, ps, g, b)


def _router_kernel(x_ref, whi_ref, wlo_ref, bias_ref, tri_ref, eid_ref, gate_ref, rank_ref, cnt_ref, run_ref):
    t = x_ref.shape[0]

    @pl.when(pl.program_id(0) == 0)
    def _():
        run_ref[...] = jnp.zeros_like(run_ref)

    x = x_ref[...]
    xh = x.astype(BF16)
    xl = (x - xh.astype(F32)).astype(BF16)
    lg = (lax.dot_general(whi_ref[...], xh, _NT, preferred_element_type=F32)
          + lax.dot_general(whi_ref[...], xl, _NT, preferred_element_type=F32)
          + lax.dot_general(wlo_ref[...], xh, _NT, preferred_element_type=F32)
          + bias_ref[...])
    ridx = lax.broadcasted_iota(I32, (N_GROUPS, t), 0)

    g = lg[0:N_GROUPS]
    gmax = jnp.max(g, axis=0, keepdims=True)
    g_sel = jnp.min(jnp.where(g == gmax, ridx, N_GROUPS), axis=0, keepdims=True)
    g_gate = 1.0 / jnp.sum(jnp.exp(g - gmax), axis=0, keepdims=True)

    e_all = lg[N_GROUPS:N_GROUPS + N_EXPERTS]
    e = jnp.zeros((EXP_PER_GROUP, t), F32)
    for gi in range(N_GROUPS):
        e = e + jnp.where(g_sel == gi, e_all[gi * EXP_PER_GROUP:(gi + 1) * EXP_PER_GROUP], 0.0)
    pe = jnp.exp(e - jnp.max(e, axis=0, keepdims=True))
    p1 = jnp.max(pe, axis=0, keepdims=True)
    i1 = jnp.min(jnp.where(pe == p1, ridx, EXP_PER_GROUP), axis=0, keepdims=True)
    pe2 = jnp.where(ridx == i1, -1.0, pe)
    p2 = jnp.max(pe2, axis=0, keepdims=True)
    i2 = jnp.min(jnp.where(pe2 == p2, ridx, EXP_PER_GROUP), axis=0, keepdims=True)
    denom = p1 + p2
    e1 = g_sel * EXP_PER_GROUP + i1
    e2 = g_sel * EXP_PER_GROUP + i2
    eid_ref[0:1, :] = e1
    eid_ref[1:2, :] = e2
    gate_ref[0:1, :] = g_gate * p1 / denom
    gate_ref[1:2, :] = g_gate * p2 / denom

    eidx = lax.broadcasted_iota(I32, (N_EXPERTS, t), 0)
    oh1 = eidx == e1
    oh2 = eidx == e2
    oh = jnp.where(oh1 | oh2, 1.0, 0.0)
    earlier = jnp.dot(oh.astype(BF16), tri_ref[...], preferred_element_type=F32)
    tot = earlier + run_ref[:, 0:1]
    rank_ref[0:1, :] = jnp.sum(jnp.where(oh1, tot, 0.0), axis=0, keepdims=True).astype(I32)
    rank_ref[1:2, :] = jnp.sum(jnp.where(oh2, tot, 0.0), axis=0, keepdims=True).astype(I32)
    run_ref[...] = run_ref[...] + jnp.sum(oh, axis=1, keepdims=True)
    cnt_ref[...] = run_ref[...].astype(I32)


def _router(x2d, w_hi, w_lo, bias, tri):
    n, d = x2d.shape
    t = ROUTE_TILE
    tok = lambda dt: jax.ShapeDtypeStruct((TOP_K, n), dt)
    tok_spec = pl.BlockSpec((TOP_K, t), lambda i: (0, i))
    return pl.pallas_call(
        _router_kernel,
        grid=(n // t,),
        in_specs=[
            pl.BlockSpec((t, d), lambda i: (i, 0)),
            pl.BlockSpec(w_hi.shape, lambda i: (0, 0)),
            pl.BlockSpec(w_lo.shape, lambda i: (0, 0)),
            pl.BlockSpec(bias.shape, lambda i: (0, 0)),
            pl.BlockSpec(tri.shape, lambda i: (0, 0)),
        ],
        out_specs=[tok_spec, tok_spec, tok_spec, pl.BlockSpec((N_EXPERTS, LANES), lambda i: (0, 0))],
        out_shape=[tok(I32), tok(F32), tok(I32), jax.ShapeDtypeStruct((N_EXPERTS, LANES), I32)],
        scratch_shapes=[pltpu.VMEM((N_EXPERTS, LANES), F32)],
        compiler_params=pltpu.CompilerParams(dimension_semantics=("arbitrary",)),
        name="moe_router",
    )(x2d, w_hi, w_lo, bias, tri)


def _row_copy(src, src_row, dst, dst_row, sem):
    return pltpu.make_async_copy(src.at[pl.ds(src_row, 1)], dst.at[pl.ds(dst_row, 1)], sem)


def _dispatch_kernel(pstart_ref, eid_ref, rank_ref, x_ref, rows_in_hbm, rows_hbm, dest_ref, sem):
    del rows_in_hbm
    t = x_ref.shape[0]

    def issue(r, c):
        for k in range(TOP_K):
            slot = pstart_ref[eid_ref[k, r]] + rank_ref[k, r]
            dest_ref[k, r] = slot
            _row_copy(x_ref, r, rows_hbm, slot, sem).start()
        return c

    def drain(r, c):
        for k in range(TOP_K):
            _row_copy(x_ref, 0, rows_hbm, 0, sem).wait()
        return c

    lax.fori_loop(0, t, issue, 0, unroll=8)
    lax.fori_loop(0, t, drain, 0, unroll=8)


def _dispatch(pstart, eid, rank, x2d, rows_init):
    n, d = x2d.shape
    t = DISPATCH_TILE
    tok_spec = pl.BlockSpec((TOP_K, t), lambda i, ps: (0, i), memory_space=pltpu.SMEM)
    return pl.pallas_call(
        _dispatch_kernel,
        grid_spec=pltpu.PrefetchScalarGridSpec(
            num_scalar_prefetch=1,
            grid=(n // t,),
            in_specs=[
                tok_spec, tok_spec,
                pl.BlockSpec((t, d), lambda i, ps: (i, 0)),
                pl.BlockSpec(memory_space=pl.ANY),
            ],
            out_specs=[pl.BlockSpec(memory_space=pl.ANY), tok_spec],
            scratch_shapes=[pltpu.SemaphoreType.DMA(())],
        ),
        out_shape=[jax.ShapeDtypeStruct(rows_init.shape, rows_init.dtype),
                   jax.ShapeDtypeStruct((TOP_K, n), I32)],
        input_output_aliases={4: 0},
        compiler_params=pltpu.CompilerParams(dimension_semantics=("arbitrary",)),
        name="moe_dispatch",
    )(pstart, eid, rank, x2d, rows_init)


def _ffn_kernel(be_ref, nu_ref, rows_ref, w1_ref, w3_ref, w2_ref, y_ref, w1b, w3b, w2b):
    b = pl.program_id(0)

    @pl.when(b < nu_ref[0])
    def _():
        prev = be_ref[jnp.maximum(b - 1, 0)]

        @pl.when((b == 0) | (be_ref[b] != prev))
        def _():
            w1b[...] = w1_ref[0, 0].astype(BF16)
            w3b[...] = w3_ref[0, 0].astype(BF16)
            w2b[...] = w2_ref[0, 0].astype(BF16)

        xb = rows_ref[...].astype(BF16)
        h1 = jnp.dot(xb, w1b[...], preferred_element_type=F32)
        h3 = jnp.dot(xb, w3b[...], preferred_element_type=F32)
        hb = h1 * jax.nn.sigmoid(h1) * h3
        y_ref[...] = jnp.dot(hb.astype(BF16), w2b[...], preferred_element_type=F32)


def _expert_ffn(blk_e, n_used, rows, w1, w3, w2, layer):
    r, d = rows.shape
    de = w1.shape[3]
    bm = EXPERT_BLOCK
    nb = r // bm
    clamp = lambda b, nu: jnp.minimum(b, nu[0] - 1)
    row_spec = pl.BlockSpec((bm, d), lambda b, be, nu: (clamp(b, nu), 0))
    w_idx = lambda b, be, nu: (layer, be[clamp(b, nu)], 0, 0)
    return pl.pallas_call(
        _ffn_kernel,
        grid_spec=pltpu.PrefetchScalarGridSpec(
            num_scalar_prefetch=2,
            grid=(nb,),
            in_specs=[
                row_spec,
                pl.BlockSpec((1, 1, d, de), w_idx),
                pl.BlockSpec((1, 1, d, de), w_idx),
                pl.BlockSpec((1, 1, de, d), w_idx),
            ],
            out_specs=row_spec,
            scratch_shapes=[pltpu.VMEM((d, de), BF16), pltpu.VMEM((d, de), BF16), pltpu.VMEM((de, d), BF16)],
        ),
        out_shape=jax.ShapeDtypeStruct((r, d), F32),
        compiler_params=pltpu.CompilerParams(dimension_semantics=("arbitrary",)),
        name="moe_expert_ffn",
    )(blk_e, n_used, rows, w1, w3, w2)


def _combine_ln_kernel(dest_ref, gate_ref, x_ref, y_hbm, g_ref, b_ref, out_ref, ybuf, sem, *, alpha):
    t = x_ref.shape[0]

    def issue(r, c):
        for k in range(TOP_K):
            _row_copy(y_hbm, dest_ref[k, r], ybuf.at[k], r, sem).start()
        return c

    def drain(r, c):
        for k in range(TOP_K):
            _row_copy(y_hbm, 0, ybuf.at[k], 0, sem).wait()
        return c

    lax.fori_loop(0, t, issue, 0, unroll=8)
    lax.fori_loop(0, t, drain, 0, unroll=8)
    gate = gate_ref[...]
    y = alpha * x_ref[...] + gate[:, 0:1] * ybuf[0] + gate[:, 1:2] * ybuf[1]
    out_ref[...] = _layer_norm(y, g_ref[...], b_ref[...])


def _combine_ln(dest, gates_tok, x2d, y_rows, g, b, alpha):
    n, d = x2d.shape
    t = COMBINE_TILE
    vec = pl.BlockSpec((1, d), lambda i: (0, 0))
    return pl.pallas_call(
        functools.partial(_combine_ln_kernel, alpha=alpha),
        grid=(n // t,),
        in_specs=[
            pl.BlockSpec((TOP_K, t), lambda i: (0, i), memory_space=pltpu.SMEM),
            pl.BlockSpec((t, TOP_K), lambda i: (i, 0)),
            pl.BlockSpec((t, d), lambda i: (i, 0)),
            pl.BlockSpec(memory_space=pl.ANY),
            vec, vec,
        ],
        out_specs=pl.BlockSpec((t, d), lambda i: (i, 0)),
        out_shape=jax.ShapeDtypeStruct((n, d), F32),
        scratch_shapes=[pltpu.VMEM((TOP_K, t, d), F32), pltpu.SemaphoreType.DMA(())],
        compiler_params=pltpu.CompilerParams(dimension_semantics=("arbitrary",)),
        name="moe_combine_ln",
    )(dest, gates_tok, x2d, y_rows, g, b)


def _rope_tables(positions):
    inv_freq = ROPE_THETA ** (-jnp.arange(0, QK_ROPE, 2, dtype=F32) / QK_ROPE)
    ang = positions.astype(F32)[..., None] * inv_freq
    cos, sin = jnp.cos(ang), jnp.sin(ang)
    b, s, _ = cos.shape
    zeros = jnp.zeros((b * s, QK_NOPE), F32)
    tail = jnp.zeros((b * s, LANES - QK_NOPE - QK_ROPE), F32)
    c2, s2 = cos.reshape(b * s, -1), sin.reshape(b * s, -1)
    return {
        'cos_t': jnp.swapaxes(cos, 1, 2), 'sin_t': jnp.swapaxes(sin, 1, 2),
        'cos_pad': jnp.concatenate([zeros, c2, c2, tail], axis=1),
        'sin_pad': jnp.concatenate([zeros, s2, s2, tail], axis=1),
    }


def _mla_weights(w_in, q_norm, w_uq, kv_norm, w_ukv, w_o):
    d = w_in.shape[0]
    lat_q_kv = Q_LORA + KV_LORA
    r1 = w_in[:, lat_q_kv:lat_q_kv + HALF_ROPE]
    r2 = w_in[:, lat_q_kv + HALF_ROPE:lat_q_kv + QK_ROPE]
    z_lo = jnp.zeros((d, QK_NOPE), F32)
    z_hi = jnp.zeros((d, LANES - QK_NOPE - QK_ROPE), F32)
    w_tok = jnp.concatenate([w_in[:, Q_LORA:lat_q_kv],
                             z_lo, r1, r2, z_hi,
                             z_lo, -r2, r1, z_hi],
                            axis=1)
    uq = w_uq.reshape(Q_LORA, N_HEADS, QK_NOPE + QK_ROPE)
    w_uq_t = jnp.concatenate([uq[:, :, :QK_NOPE].reshape(Q_LORA, -1),
                              uq[:, :, QK_NOPE:QK_NOPE + HALF_ROPE].reshape(Q_LORA, -1),
                              uq[:, :, QK_NOPE + HALF_ROPE:].reshape(Q_LORA, -1)], axis=1).T
    ukv = w_ukv.reshape(KV_LORA, N_HEADS, QK_NOPE + V_DIM)
    w_kn = jnp.concatenate([ukv[:, :, :QK_NOPE], jnp.zeros((KV_LORA, N_HEADS, HEAD_PAD - QK_NOPE), F32)],
                           axis=2).reshape(KV_LORA, N_HEADS * HEAD_PAD)
    w_v_t = ukv[:, :, QK_NOPE:].reshape(KV_LORA, N_HEADS * V_DIM).T
    return {
        'w_fm': w_in[:, :lat_q_kv].T.astype(BF16), 'w_tok': w_tok.astype(BF16),
        'w_uq_t': w_uq_t.astype(BF16), 'w_v_t': w_v_t.astype(BF16), 'w_kn': w_kn.astype(BF16),
        'qn_col': q_norm.reshape(-1, 1), 'kvn_col': kv_norm.reshape(-1, 1), 'kvn_row': kv_norm.reshape(1, -1),
        'w_o': w_o.astype(BF16),
    }


def _router_weights(w_grp, b_grp, w_exp, b_exp):
    d = w_grp.shape[0]
    pad = LANES - N_GROUPS - N_EXPERTS
    w = jnp.concatenate([w_grp, w_exp, jnp.zeros((d, pad), F32)], axis=1).T
    w_hi = w.astype(BF16)
    w_lo = (w - w_hi.astype(F32)).astype(BF16)
    bias = jnp.concatenate([b_grp, b_exp, jnp.zeros((pad,), F32)]).reshape(-1, 1)
    return w_hi, w_lo, bias


def _moe_ln(x2d, w_hi, w_lo, bias, tri, w1, w3, w2, layer, g, b, alpha):
    n, d = x2d.shape
    bm = EXPERT_BLOCK
    n_assign = n * TOP_K
    nb = -(-(n_assign + N_EXPERTS * (bm - 1)) // bm)
    eid, gates, rank, cnt = _router(x2d, w_hi, w_lo, bias, tri)
    counts = cnt[:, 0]
    padded = (counts + bm - 1) // bm * bm
    pend = jnp.cumsum(padded)
    pstart = (pend - padded).astype(I32)
    n_used = (pend[-1:] // bm).astype(I32)
    blk_start = jnp.arange(nb, dtype=I32) * bm
    blk_e = jnp.minimum(jnp.sum((pend[None, :] <= blk_start[:, None]).astype(I32), axis=1), N_EXPERTS - 1)
    rows, dest = _dispatch(pstart, eid, rank, x2d, jnp.zeros((nb * bm, d), F32))
    y_rows = _expert_ffn(blk_e, n_used, rows, w1, w3, w2, layer)
    return _combine_ln(dest, gates.T, x2d, y_rows, g, b, alpha)


def kernel(x, positions, ln_g, ln_b, mla_w_in, mla_q_norm, mla_w_uq, mla_kv_norm, mla_w_ukv, mla_w_o,
           pool_w, pool_b, pool_scale, moe_w_grp, moe_b_grp, moe_w_exp, moe_b_exp, moe_w1, moe_w3, moe_w2):
    batch, seq, d = x.shape
    depth = ln_g.shape[0]
    alpha = (2 * depth) ** 0.25
    tabs = _rope_tables(positions)
    t = ROUTE_TILE
    tri = (lax.broadcasted_iota(I32, (t, t), 0) < lax.broadcasted_iota(I32, (t, t), 1)).astype(BF16)
    x2d = x.reshape(batch * seq, d)
    vec = lambda v: v.reshape(1, d)
    for i in range(depth):
        j = i // 2
        if i % 2 == 0:
            w = _mla_weights(mla_w_in[j], mla_q_norm[j], mla_w_uq[j], mla_kv_norm[j], mla_w_ukv[j], mla_w_o[j])
            q_t, k, v_t = _mla_proj(x2d, tabs, w, batch, seq)
            o_t = _attention(q_t, k, v_t)
            x2d = _outproj_ln(o_t, x2d, w['w_o'], vec(ln_g[i, 0]), vec(ln_b[i, 0]), alpha)
        else:
            x2d = _pool_ln(x2d, seq, pool_w[j].astype(BF16), vec(pool_b[j]), vec(pool_scale[j]),
                           vec(ln_g[i, 0]), vec(ln_b[i, 0]), alpha)
        w_hi, w_lo, bias = _router_weights(moe_w_grp[i], moe_b_grp[i], moe_w_exp[i], moe_b_exp[i])
        x2d = _moe_ln(x2d, w_hi, w_lo, bias, tri, moe_w1, moe_w3, moe_w2, i,
                      vec(ln_g[i, 1]), vec(ln_b[i, 1]), alpha)
    return x2d.reshape(batch, seq, d)
```

```python
import functools
import math

import jax
import jax.numpy as jnp
from jax import lax
from jax.experimental import pallas as pl
from jax.experimental.pallas import tpu as pltpu

F32 = jnp.float32
BF16 = jnp.bfloat16
I32 = jnp.int32

N_HEADS = 16
QK_NOPE = 64
QK_ROPE = 32
HALF_ROPE = QK_ROPE // 2
V_DIM = 64
Q_LORA = 256
KV_LORA = 128
ROPE_THETA = 10000.0
POOL_WINDOWS = (2, 4, 8, 16)
N_GROUPS = 8
EXP_PER_GROUP = 8
N_EXPERTS = N_GROUPS * EXP_PER_GROUP
TOP_K = 2
LN_EPS = 1e-5
RMS_EPS = 1e-6

LANES = 128
HEAD_PAD = 128
V_ROWS = 80
POOL_HALO = 16

TOK_TILE = 512
ATT_TQ = 1024
ATT_TK = 512
ATT_HEADS_PER_STEP = 1
ROUTE_TILE = 512
DISPATCH_TILE = 512
COMBINE_TILE = 256
EXPERT_BLOCK = 256

NEG_BIG = -1e30
_NT = (((1,), (1,)), ((), ()))
_TN = (((0,), (0,)), ((), ()))


def _layer_norm(y, g, b):
    mu = jnp.mean(y, axis=-1, keepdims=True)
    yc = y - mu
    var = jnp.mean(yc * yc, axis=-1, keepdims=True)
    return yc * lax.rsqrt(var + LN_EPS) * g + b


def _mla_proj_kernel(x_ref, cost_ref, sint_ref, cosp_ref, sinp_ref, wfm_ref, wtok_ref, wuq_ref,
                     wv_ref, wkn_ref, qn_ref, kvnc_ref, kvnr_ref, qt_ref, k_ref, vt_ref, *, q_scale):
    t = x_ref.shape[0]
    xb = x_ref[...].astype(BF16)
    lat_t = lax.dot_general(wfm_ref[...], xb, _NT, preferred_element_type=F32)
    cq = lat_t[0:Q_LORA]
    cq = cq * lax.rsqrt(jnp.mean(cq * cq, axis=0, keepdims=True) + RMS_EPS) * qn_ref[...]
    ckv = lat_t[Q_LORA:Q_LORA + KV_LORA]
    ckv = ckv * lax.rsqrt(jnp.mean(ckv * ckv, axis=0, keepdims=True) + RMS_EPS) * kvnc_ref[...]

    q_t = jnp.dot(wuq_ref[...], cq.astype(BF16), preferred_element_type=F32) * q_scale
    n_nope = N_HEADS * QK_NOPE
    n_half = N_HEADS * HALF_ROPE
    qn = q_t[0:n_nope].reshape(N_HEADS, QK_NOPE, t)
    x1 = q_t[n_nope:n_nope + n_half].reshape(N_HEADS, HALF_ROPE, t)
    x2 = q_t[n_nope + n_half:n_nope + 2 * n_half].reshape(N_HEADS, HALF_ROPE, t)
    c = cost_ref[...]
    s = sint_ref[...]
    qt_ref[0, :, 0:QK_NOPE, :] = qn.astype(BF16)
    qt_ref[0, :, QK_NOPE:QK_NOPE + HALF_ROPE, :] = (x1 * c - x2 * s).astype(BF16)
    qt_ref[0, :, QK_NOPE + HALF_ROPE:QK_NOPE + QK_ROPE, :] = (x2 * c + x1 * s).astype(BF16)
    qt_ref[0, :, QK_NOPE + QK_ROPE:HEAD_PAD, :] = jnp.zeros(
        (N_HEADS, HEAD_PAD - QK_NOPE - QK_ROPE, t), BF16)

    v_t = jnp.dot(wv_ref[...], ckv.astype(BF16), preferred_element_type=F32)
    vt_ref[0, :, 0:V_DIM, :] = v_t.reshape(N_HEADS, V_DIM, t).astype(BF16)
    vt_ref[0, :, V_DIM:V_ROWS, :] = jnp.ones((N_HEADS, V_ROWS - V_DIM, t), BF16)

    lat = jnp.dot(xb, wtok_ref[...], preferred_element_type=F32)
    ckv_tok = lat[:, 0:KV_LORA]
    ckv_tok = ckv_tok * lax.rsqrt(jnp.mean(ckv_tok * ckv_tok, axis=-1, keepdims=True) + RMS_EPS) * kvnr_ref[...]
    k_rope = lat[:, KV_LORA:KV_LORA + LANES] * cosp_ref[...] + lat[:, KV_LORA + LANES:] * sinp_ref[...]
    kn = jnp.dot(ckv_tok.astype(BF16), wkn_ref[...], preferred_element_type=F32)
    for h in range(N_HEADS):
        k_ref[0, h] = (kn[:, h * HEAD_PAD:(h + 1) * HEAD_PAD] + k_rope).astype(BF16)


def _mla_proj(x2d, tabs, w, batch, seq):
    n, d = x2d.shape
    t = TOK_TILE
    tiles_per_seq = seq // t
    q_scale = (QK_NOPE + QK_ROPE) ** -0.5 * math.log2(math.e)
    full = lambda a: pl.BlockSpec(a.shape, lambda i: (0,) * a.ndim)
    bs_idx = lambda i: (i // tiles_per_seq, 0, 0, i % tiles_per_seq)
    return pl.pallas_call(
        functools.partial(_mla_proj_kernel, q_scale=q_scale),
        grid=(n // t,),
        in_specs=[
            pl.BlockSpec((t, d), lambda i: (i, 0)),
            pl.BlockSpec((1, HALF_ROPE, t), lambda i: (i // tiles_per_seq, 0, i % tiles_per_seq)),
            pl.BlockSpec((1, HALF_ROPE, t), lambda i: (i // tiles_per_seq, 0, i % tiles_per_seq)),
            pl.BlockSpec((t, LANES), lambda i: (i, 0)),
            pl.BlockSpec((t, LANES), lambda i: (i, 0)),
            full(w['w_fm']), full(w['w_tok']), full(w['w_uq_t']), full(w['w_v_t']), full(w['w_kn']),
            full(w['qn_col']), full(w['kvn_col']), full(w['kvn_row']),
        ],
        out_specs=[
            pl.BlockSpec((1, N_HEADS, HEAD_PAD, t), bs_idx),
            pl.BlockSpec((1, N_HEADS, t, HEAD_PAD), lambda i: (i // tiles_per_seq, 0, i % tiles_per_seq, 0)),
            pl.BlockSpec((1, N_HEADS, V_ROWS, t), bs_idx),
        ],
        out_shape=[
            jax.ShapeDtypeStruct((batch, N_HEADS, HEAD_PAD, seq), BF16),
            jax.ShapeDtypeStruct((batch, N_HEADS, seq, HEAD_PAD), BF16),
            jax.ShapeDtypeStruct((batch, N_HEADS, V_ROWS, seq), BF16),
        ],
        compiler_params=pltpu.CompilerParams(dimension_semantics=("arbitrary",),
                                             vmem_limit_bytes=56 * 1024 * 1024),
        name="mla_proj",
    )(x2d, tabs['cos_t'], tabs['sin_t'], tabs['cos_pad'], tabs['sin_pad'],
      w['w_fm'], w['w_tok'], w['w_uq_t'], w['w_v_t'], w['w_kn'], w['qn_col'], w['kvn_col'], w['kvn_row'])


def _attn_kernel(qt_ref, k_ref, vt_ref, o_ref, s_scr, p_scr):
    n_h = qt_ref.shape[1]
    seq = k_ref.shape[2]
    tq, tk = ATT_TQ, ATT_TK
    n_q = seq // tq
    heads = range(n_h)

    def load_q(q0):
        return [qt_ref[0, h, :, pl.ds(q0, tq)] for h in heads]

    def scores(h, q_t, k0, slot):
        kb = k_ref[0, h, pl.ds(k0, tk), :]
        s = jnp.dot(kb, q_t, preferred_element_type=F32)
        s_scr[h, slot] = s
        return jnp.max(s, axis=0, keepdims=True)

    def softmax(h, slot, m, bmax, diag_off):
        s = s_scr[h, slot]
        if diag_off is not None:
            row = lax.broadcasted_iota(I32, (tk, tq), 0) + diag_off
            col = lax.broadcasted_iota(I32, (tk, tq), 1)
            s = jnp.where(row <= col, s, NEG_BIG)
            bmax = jnp.max(s, axis=0, keepdims=True)
        m_new = jnp.maximum(m, bmax)
        p_scr[h, slot] = jnp.exp2(s - m_new).astype(BF16)
        return m_new, jnp.exp2(m - m_new)

    def values(h, k0, slot, alpha, acc):
        vb = vt_ref[0, h, :, pl.ds(k0, tk)]
        return alpha * acc + jnp.dot(vb, p_scr[h, slot], preferred_element_type=F32)

    def trip(t, carry, q_ts, q_next):
        last = q_next is not None
        m, acc, a_pend, bmax0 = [list(c) for c in carry]
        ka = pl.multiple_of(t * tq, tq)
        k_prev = pl.multiple_of(jnp.maximum(ka - tk, 0), tk)
        a0, bmax1 = [None] * n_h, [None] * n_h
        for h in heads:
            bmax1[h] = scores(h, q_ts[h], ka + tk, 1)
        for h in heads:
            m[h], a0[h] = softmax(h, 0, m[h], bmax0[h], 0 if last else None)
        for h in heads:
            acc[h] = values(h, k_prev, 1, a_pend[h], acc[h])
        for h in heads:
            bmax0[h] = scores(h, q_next[h], 0, 0) if last else scores(h, q_ts[h], ka + tq, 0)
        for h in heads:
            m[h], a_pend[h] = softmax(h, 1, m[h], bmax1[h], tk if last else None)
        for h in heads:
            acc[h] = values(h, ka, 0, a0[h], acc[h])
        return tuple(m), tuple(acc), tuple(a_pend), tuple(bmax0)

    def q_tile(qi, bmax0):
        q0 = pl.multiple_of(qi * tq, tq)
        q_ts = load_q(q0)
        q_next = load_q(pl.multiple_of(jnp.minimum(q0 + tq, seq - tq), tq))
        for h in heads:
            p_scr[h, 1] = jnp.zeros((tk, tq), BF16)
        init = (tuple(jnp.full((1, tq), NEG_BIG, F32) for _ in heads),
                tuple(jnp.zeros((V_ROWS, tq), F32) for _ in heads),
                tuple(jnp.ones((1, tq), F32) for _ in heads),
                bmax0)
        res = lax.fori_loop(0, qi, lambda t, c: trip(t, c, q_ts, None), init)
        m, acc, a_pend, bmax0 = trip(qi, res, q_ts, q_next)
        for h in heads:
            out = values(h, q0 + tk, 1, a_pend[h], acc[h])
            o_ref[0, h * V_DIM:(h + 1) * V_DIM, pl.ds(q0, tq)] = (
                out[0:V_DIM] / out[V_DIM:V_DIM + 1]).astype(BF16)
        return bmax0

    q_first = load_q(0)
    lax.fori_loop(0, n_q, q_tile, tuple(scores(h, q_first[h], 0, 0) for h in heads))


def _attention(q_t, k, v_t):
    batch, heads, _, seq = q_t.shape
    g = ATT_HEADS_PER_STEP
    assert ATT_TQ == 2 * ATT_TK and seq % ATT_TQ == 0 and heads % g == 0
    return pl.pallas_call(
        _attn_kernel,
        grid=(batch, heads // g),
        in_specs=[
            pl.BlockSpec((1, g, HEAD_PAD, seq), lambda b, h: (b, h, 0, 0)),
            pl.BlockSpec((1, g, seq, HEAD_PAD), lambda b, h: (b, h, 0, 0)),
            pl.BlockSpec((1, g, V_ROWS, seq), lambda b, h: (b, h, 0, 0)),
        ],
        out_specs=pl.BlockSpec((1, g * V_DIM, seq), lambda b, h: (b, h, 0)),
        out_shape=jax.ShapeDtypeStruct((batch, heads * V_DIM, seq), BF16),
        scratch_shapes=[pltpu.VMEM((g, 2, ATT_TK, ATT_TQ), F32), pltpu.VMEM((g, 2, ATT_TK, ATT_TQ), BF16)],
        compiler_params=pltpu.CompilerParams(dimension_semantics=("arbitrary", "arbitrary"),
                                             vmem_limit_bytes=56 * 1024 * 1024),
        name="mla_attention",
    )(q_t, k, v_t)


def _outproj_ln_kernel(ot_ref, x_ref, wo_ref, g_ref, b_ref, out_ref, *, alpha):
    f = lax.dot_general(ot_ref[0], wo_ref[...], _TN, preferred_element_type=F32)
    out_ref[...] = _layer_norm(alpha * x_ref[...] + f, g_ref[...], b_ref[...])


def _outproj_ln(o_t, x2d, w_o, g, b, alpha):
    n, d = x2d.shape
    batch, hd, seq = o_t.shape
    t = TOK_TILE
    tiles_per_seq = seq // t
    return pl.pallas_call(
        functools.partial(_outproj_ln_kernel, alpha=alpha),
        grid=(n // t,),
        in_specs=[
            pl.BlockSpec((1, hd, t), lambda i: (i // tiles_per_seq, 0, i % tiles_per_seq)),
            pl.BlockSpec((t, d), lambda i: (i, 0)),
            pl.BlockSpec((hd, d), lambda i: (0, 0)),
            pl.BlockSpec((1, d), lambda i: (0, 0)),
            pl.BlockSpec((1, d), lambda i: (0, 0)),
        ],
        out_specs=pl.BlockSpec((t, d), lambda i: (i, 0)),
        out_shape=jax.ShapeDtypeStruct((n, d), F32),
        compiler_params=pltpu.CompilerParams(dimension_semantics=("arbitrary",)),
        name="mla_outproj_ln",
    )(o_t, x2d, w_o, g, b)


def _pool_ln_kernel(x_ref, halo_ref, w_ref, pb_ref, ps_ref, g_ref, b_ref, out_ref, *, alpha, tiles_per_seq):
    t, d = x_ref.shape
    gd = d // len(POOL_WINDOWS)
    i = pl.program_id(0)
    tile_in_seq = i % tiles_per_seq
    x = x_ref[...]
    halo = jnp.where(tile_in_seq == 0, 0.0, halo_ref[...])
    ext = jnp.concatenate([halo, x], axis=0)
    pos = tile_in_seq * t + lax.broadcasted_iota(I32, (t, 1), 0)
    ys = []
    for gi, win in enumerate(POOL_WINDOWS):
        e = ext[:, gi * gd:(gi + 1) * gd]
        sh = 1
        while sh < win:
            e = e + pltpu.roll(e, sh, axis=0)
            sh *= 2
        cnt = jnp.minimum(pos + 1, win).astype(F32)
        pooled = e[POOL_HALO:] / cnt - x[:, gi * gd:(gi + 1) * gd]
        ys.append(jnp.dot(pooled.astype(BF16), w_ref[gi], preferred_element_type=F32))
    y = (jnp.concatenate(ys, axis=1) + pb_ref[...]) * ps_ref[...]
    out_ref[...] = _layer_norm(alpha * x + y, g_ref[...], b_ref[...])


def _pool_ln(x2d, seq, w, pb, ps, g, b, alpha):
    n, d = x2d.shape
    t = TOK_TILE
    tiles_per_seq = seq // t
    halo_blocks = t // POOL_HALO
    vec = pl.BlockSpec((1, d), lambda i: (0, 0))
    return pl.pallas_call(
        functools.partial(_pool_ln_kernel, alpha=alpha, tiles_per_seq=tiles_per_seq),
        grid=(n // t,),
        in_specs=[
            pl.BlockSpec((t, d), lambda i: (i, 0)),
            pl.BlockSpec((POOL_HALO, d), lambda i: (jnp.maximum(i * halo_blocks - 1, 0), 0)),
            pl.BlockSpec(w.shape, lambda i: (0, 0, 0)),
            vec, vec, vec, vec,
        ],
        out_specs=pl.BlockSpec((t, d), lambda i: (i, 0)),
        out_shape=jax.ShapeDtypeStruct((n, d), F32),
        compiler_params=pltpu.CompilerParams(dimension_semantics=("arbitrary",)),
        name="pool_ln",
    )(x2d, x2d, w, pb, ps, g, b)


def _router_kernel(x_ref, whi_ref, wlo_ref, bias_ref, tri_ref, eid_ref, gate_ref, rank_ref, cnt_ref, run_ref):
    t = x_ref.shape[0]

    @pl.when(pl.program_id(0) == 0)
    def _():
        run_ref[...] = jnp.zeros_like(run_ref)

    x = x_ref[...]
    xh = x.astype(BF16)
    xl = (x - xh.astype(F32)).astype(BF16)
    lg = (lax.dot_general(whi_ref[...], xh, _NT, preferred_element_type=F32)
          + lax.dot_general(whi_ref[...], xl, _NT, preferred_element_type=F32)
          + lax.dot_general(wlo_ref[...], xh, _NT, preferred_element_type=F32)
          + bias_ref[...])
    ridx = lax.broadcasted_iota(I32, (N_GROUPS, t), 0)

    g = lg[0:N_GROUPS]
    gmax = jnp.max(g, axis=0, keepdims=True)
    g_sel = jnp.min(jnp.where(g == gmax, ridx, N_GROUPS), axis=0, keepdims=True)
    g_gate = 1.0 / jnp.sum(jnp.exp(g - gmax), axis=0, keepdims=True)

    e_all = lg[N_GROUPS:N_GROUPS + N_EXPERTS]
    e = jnp.zeros((EXP_PER_GROUP, t), F32)
    for gi in range(N_GROUPS):
        e = e + jnp.where(g_sel == gi, e_all[gi * EXP_PER_GROUP:(gi + 1) * EXP_PER_GROUP], 0.0)
    pe = jnp.exp(e - jnp.max(e, axis=0, keepdims=True))
    p1 = jnp.max(pe, axis=0, keepdims=True)
    i1 = jnp.min(jnp.where(pe == p1, ridx, EXP_PER_GROUP), axis=0, keepdims=True)
    pe2 = jnp.where(ridx == i1, -1.0, pe)
    p2 = jnp.max(pe2, axis=0, keepdims=True)
    i2 = jnp.min(jnp.where(pe2 == p2, ridx, EXP_PER_GROUP), axis=0, keepdims=True)
    denom = p1 + p2
    e1 = g_sel * EXP_PER_GROUP + i1
    e2 = g_sel * EXP_PER_GROUP + i2
    eid_ref[0:1, :] = e1
    eid_ref[1:2, :] = e2
    gate_ref[0:1, :] = g_gate * p1 / denom
    gate_ref[1:2, :] = g_gate * p2 / denom

    eidx = lax.broadcasted_iota(I32, (N_EXPERTS, t), 0)
    oh1 = eidx == e1
    oh2 = eidx == e2
    oh = jnp.where(oh1 | oh2, 1.0, 0.0)
    earlier = jnp.dot(oh.astype(BF16), tri_ref[...], preferred_element_type=F32)
    tot = earlier + run_ref[:, 0:1]
    rank_ref[0:1, :] = jnp.sum(jnp.where(oh1, tot, 0.0), axis=0, keepdims=True).astype(I32)
    rank_ref[1:2, :] = jnp.sum(jnp.where(oh2, tot, 0.0), axis=0, keepdims=True).astype(I32)
    run_ref[...] = run_ref[...] + jnp.sum(oh, axis=1, keepdims=True)
    cnt_ref[...] = run_ref[...].astype(I32)


def _router(x2d, w_hi, w_lo, bias, tri):
    n, d = x2d.shape
    t = ROUTE_TILE
    tok = lambda dt: jax.ShapeDtypeStruct((TOP_K, n), dt)
    tok_spec = pl.BlockSpec((TOP_K, t), lambda i: (0, i))
    return pl.pallas_call(
        _router_kernel,
        grid=(n // t,),
        in_specs=[
            pl.BlockSpec((t, d), lambda i: (i, 0)),
            pl.BlockSpec(w_hi.shape, lambda i: (0, 0)),
            pl.BlockSpec(w_lo.shape, lambda i: (0, 0)),
            pl.BlockSpec(bias.shape, lambda i: (0, 0)),
            pl.BlockSpec(tri.shape, lambda i: (0, 0)),
        ],
        out_specs=[tok_spec, tok_spec, tok_spec, pl.BlockSpec((N_EXPERTS, LANES), lambda i: (0, 0))],
        out_shape=[tok(I32), tok(F32), tok(I32), jax.ShapeDtypeStruct((N_EXPERTS, LANES), I32)],
        scratch_shapes=[pltpu.VMEM((N_EXPERTS, LANES), F32)],
        compiler_params=pltpu.CompilerParams(dimension_semantics=("arbitrary",)),
        name="moe_router",
    )(x2d, w_hi, w_lo, bias, tri)


ROWS_PER_TRIP = 8


def _pack_halves(y):
    half = y.shape[1] // 2
    bits = lax.bitcast_convert_type(y.astype(BF16).astype(F32), jnp.uint32)
    return bits[:, :half] | (bits[:, half:] >> 16)


def _unpack_halves(p):
    hi = lax.bitcast_convert_type(p & jnp.uint32(0xFFFF0000), F32)
    lo = lax.bitcast_convert_type(p << 16, F32)
    return hi, lo


def _for_row_groups(n_rows, body):
    def trip(g, c):
        base = g * ROWS_PER_TRIP
        for u in range(ROWS_PER_TRIP):
            body(g, u, base + u)
        return c
    lax.fori_loop(0, n_rows // ROWS_PER_TRIP, trip, 0)


def _staged_row(buf, g, u):
    return buf.at[g, pl.ds(u, 1)]


def _hbm_row(arr, row):
    return arr.at[pl.ds(row, 1)]


def _dispatch_kernel(pstart_ref, e0_ref, e1_ref, r0_ref, r1_ref, x_ref, rows_in_hbm,
                     rows_hbm, d0_ref, d1_ref, xp_scr, sem):
    del rows_in_hbm
    t = x_ref.shape[0]
    xp_scr[...] = _pack_halves(x_ref[...]).reshape(xp_scr.shape)

    def issue(g, u, r):
        for e_ref, r_ref, d_ref in ((e0_ref, r0_ref, d0_ref), (e1_ref, r1_ref, d1_ref)):
            slot = pstart_ref[e_ref[r]] + r_ref[r]
            d_ref[r] = slot
            pltpu.make_async_copy(_staged_row(xp_scr, g, u), _hbm_row(rows_hbm, slot), sem).start()

    def drain(g, u, r):
        for _ in range(TOP_K):
            pltpu.make_async_copy(_staged_row(xp_scr, 0, 0), _hbm_row(rows_hbm, 0), sem).wait()

    _for_row_groups(t, issue)
    _for_row_groups(t, drain)


def _dispatch(pstart, eid, rank, x2d, rows_init):
    n, d = x2d.shape
    t = DISPATCH_TILE
    tok_spec = pl.BlockSpec((t,), lambda i, ps: (i,), memory_space=pltpu.SMEM)
    tok_shape = jax.ShapeDtypeStruct((n,), I32)
    rows, d0, d1 = pl.pallas_call(
        _dispatch_kernel,
        grid_spec=pltpu.PrefetchScalarGridSpec(
            num_scalar_prefetch=1,
            grid=(n // t,),
            in_specs=[
                tok_spec, tok_spec, tok_spec, tok_spec,
                pl.BlockSpec((t, d), lambda i, ps: (i, 0)),
                pl.BlockSpec(memory_space=pl.ANY),
            ],
            out_specs=[pl.BlockSpec(memory_space=pl.ANY), tok_spec, tok_spec],
            scratch_shapes=[pltpu.VMEM((t // ROWS_PER_TRIP, ROWS_PER_TRIP, d // 2), jnp.uint32),
                            pltpu.SemaphoreType.DMA(())],
        ),
        out_shape=[jax.ShapeDtypeStruct(rows_init.shape, rows_init.dtype), tok_shape, tok_shape],
        input_output_aliases={6: 0},
        compiler_params=pltpu.CompilerParams(dimension_semantics=("arbitrary",)),
        name="moe_dispatch",
    )(pstart, eid[0], eid[1], rank[0], rank[1], x2d, rows_init)
    return rows, (d0, d1)


def _ffn_kernel(be_ref, nu_ref, rows_ref, w1_ref, w3_ref, w2_ref, y_ref, w1b, w3b, w2b):
    b = pl.program_id(0)
    half = rows_ref.shape[1]

    @pl.when(b < nu_ref[0])
    def _():
        prev = be_ref[jnp.maximum(b - 1, 0)]

        @pl.when((b == 0) | (be_ref[b] != prev))
        def _():
            w1b[...] = w1_ref[0, 0].astype(BF16)
            w3b[...] = w3_ref[0, 0].astype(BF16)
            w2b[...] = w2_ref[0, 0].astype(BF16)

        x_hi, x_lo = _unpack_halves(rows_ref[...])
        x_hi, x_lo = x_hi.astype(BF16), x_lo.astype(BF16)

        def up(w):
            return (jnp.dot(x_hi, w[0:half, :], preferred_element_type=F32)
                    + jnp.dot(x_lo, w[half:2 * half, :], preferred_element_type=F32))

        h1 = up(w1b)
        hb = h1 * jax.nn.sigmoid(h1) * up(w3b)
        y_ref[...] = _pack_halves(jnp.dot(hb.astype(BF16), w2b[...], preferred_element_type=F32))


def _expert_ffn(blk_e, n_used, rows, w1, w3, w2, layer):
    r, half = rows.shape
    d, de = w1.shape[2], w1.shape[3]
    bm = EXPERT_BLOCK
    nb = r // bm
    clamp = lambda b, nu: jnp.minimum(b, nu[0] - 1)
    row_spec = pl.BlockSpec((bm, half), lambda b, be, nu: (clamp(b, nu), 0))
    w_idx = lambda b, be, nu: (layer, be[clamp(b, nu)], 0, 0)
    return pl.pallas_call(
        _ffn_kernel,
        grid_spec=pltpu.PrefetchScalarGridSpec(
            num_scalar_prefetch=2,
            grid=(nb,),
            in_specs=[
                row_spec,
                pl.BlockSpec((1, 1, d, de), w_idx),
                pl.BlockSpec((1, 1, d, de), w_idx),
                pl.BlockSpec((1, 1, de, d), w_idx),
            ],
            out_specs=row_spec,
            scratch_shapes=[pltpu.VMEM((d, de), BF16), pltpu.VMEM((d, de), BF16), pltpu.VMEM((de, d), BF16)],
        ),
        out_shape=jax.ShapeDtypeStruct((r, half), jnp.uint32),
        compiler_params=pltpu.CompilerParams(dimension_semantics=("arbitrary",)),
        name="moe_expert_ffn",
    )(blk_e, n_used, rows, w1, w3, w2)


def _combine_ln_kernel(d0_ref, d1_ref, gate_ref, x_ref, y_hbm, g_ref, b_ref, out_ref, ybuf, sem, *, alpha):
    t = x_ref.shape[0]

    def issue(g, u, r):
        for k, d_ref in enumerate((d0_ref, d1_ref)):
            pltpu.make_async_copy(_hbm_row(y_hbm, d_ref[r]), _staged_row(ybuf.at[k], g, u), sem).start()

    def drain(g, u, r):
        for k in range(TOP_K):
            pltpu.make_async_copy(_hbm_row(y_hbm, 0), _staged_row(ybuf.at[k], 0, 0), sem).wait()

    _for_row_groups(t, issue)
    _for_row_groups(t, drain)
    gate = gate_ref[...]
    hi0, lo0 = _unpack_halves(ybuf[0].reshape(t, -1))
    hi1, lo1 = _unpack_halves(ybuf[1].reshape(t, -1))
    g0, g1 = gate[:, 0:1], gate[:, 1:2]
    f = jnp.concatenate([g0 * hi0 + g1 * hi1, g0 * lo0 + g1 * lo1], axis=1)
    out_ref[...] = _layer_norm(alpha * x_ref[...] + f, g_ref[...], b_ref[...])


def _combine_ln(dest, gates_tok, x2d, y_rows, g, b, alpha):
    n, d = x2d.shape
    t = COMBINE_TILE
    vec = pl.BlockSpec((1, d), lambda i: (0, 0))
    tok_spec = pl.BlockSpec((t,), lambda i: (i,), memory_space=pltpu.SMEM)
    return pl.pallas_call(
        functools.partial(_combine_ln_kernel, alpha=alpha),
        grid=(n // t,),
        in_specs=[
            tok_spec, tok_spec,
            pl.BlockSpec((t, TOP_K), lambda i: (i, 0)),
            pl.BlockSpec((t, d), lambda i: (i, 0)),
            pl.BlockSpec(memory_space=pl.ANY),
            vec, vec,
        ],
        out_specs=pl.BlockSpec((t, d), lambda i: (i, 0)),
        out_shape=jax.ShapeDtypeStruct((n, d), F32),
        scratch_shapes=[pltpu.VMEM((TOP_K, t // ROWS_PER_TRIP, ROWS_PER_TRIP, d // 2), jnp.uint32),
                        pltpu.SemaphoreType.DMA(())],
        compiler_params=pltpu.CompilerParams(dimension_semantics=("arbitrary",)),
        name="moe_combine_ln",
    )(dest[0], dest[1], gates_tok, x2d, y_rows, g, b)


def _rope_tables(positions):
    inv_freq = ROPE_THETA ** (-jnp.arange(0, QK_ROPE, 2, dtype=F32) / QK_ROPE)
    ang = positions.astype(F32)[..., None] * inv_freq
    cos, sin = jnp.cos(ang), jnp.sin(ang)
    b, s, _ = cos.shape
    zeros = jnp.zeros((b * s, QK_NOPE), F32)
    tail = jnp.zeros((b * s, LANES - QK_NOPE - QK_ROPE), F32)
    c2, s2 = cos.reshape(b * s, -1), sin.reshape(b * s, -1)
    return {
        'cos_t': jnp.swapaxes(cos, 1, 2), 'sin_t': jnp.swapaxes(sin, 1, 2),
        'cos_pad': jnp.concatenate([zeros, c2, c2, tail], axis=1),
        'sin_pad': jnp.concatenate([zeros, s2, s2, tail], axis=1),
    }


def _mla_weights(w_in, q_norm, w_uq, kv_norm, w_ukv, w_o):
    d = w_in.shape[0]
    lat_q_kv = Q_LORA + KV_LORA
    r1 = w_in[:, lat_q_kv:lat_q_kv + HALF_ROPE]
    r2 = w_in[:, lat_q_kv + HALF_ROPE:lat_q_kv + QK_ROPE]
    z_lo = jnp.zeros((d, QK_NOPE), F32)
    z_hi = jnp.zeros((d, LANES - QK_NOPE - QK_ROPE), F32)
    w_tok = jnp.concatenate([w_in[:, Q_LORA:lat_q_kv],
                             z_lo, r1, r2, z_hi,
                             z_lo, -r2, r1, z_hi],
                            axis=1)
    uq = w_uq.reshape(Q_LORA, N_HEADS, QK_NOPE + QK_ROPE)
    w_uq_t = jnp.concatenate([uq[:, :, :QK_NOPE].reshape(Q_LORA, -1),
                              uq[:, :, QK_NOPE:QK_NOPE + HALF_ROPE].reshape(Q_LORA, -1),
                              uq[:, :, QK_NOPE + HALF_ROPE:].reshape(Q_LORA, -1)], axis=1).T
    ukv = w_ukv.reshape(KV_LORA, N_HEADS, QK_NOPE + V_DIM)
    w_kn = jnp.concatenate([ukv[:, :, :QK_NOPE], jnp.zeros((KV_LORA, N_HEADS, HEAD_PAD - QK_NOPE), F32)],
                           axis=2).reshape(KV_LORA, N_HEADS * HEAD_PAD)
    w_v_t = ukv[:, :, QK_NOPE:].reshape(KV_LORA, N_HEADS * V_DIM).T
    return {
        'w_fm': w_in[:, :lat_q_kv].T.astype(BF16), 'w_tok': w_tok.astype(BF16),
        'w_uq_t': w_uq_t.astype(BF16), 'w_v_t': w_v_t.astype(BF16), 'w_kn': w_kn.astype(BF16),
        'qn_col': q_norm.reshape(-1, 1), 'kvn_col': kv_norm.reshape(-1, 1), 'kvn_row': kv_norm.reshape(1, -1),
        'w_o': w_o.astype(BF16),
    }


def _router_weights(w_grp, b_grp, w_exp, b_exp):
    d = w_grp.shape[0]
    pad = LANES - N_GROUPS - N_EXPERTS
    w = jnp.concatenate([w_grp, w_exp, jnp.zeros((d, pad), F32)], axis=1).T
    w_hi = w.astype(BF16)
    w_lo = (w - w_hi.astype(F32)).astype(BF16)
    bias = jnp.concatenate([b_grp, b_exp, jnp.zeros((pad,), F32)]).reshape(-1, 1)
    return w_hi, w_lo, bias


def _moe_ln(x2d, w_hi, w_lo, bias, tri, w1, w3, w2, layer, g, b, alpha):
    n, d = x2d.shape
    bm = EXPERT_BLOCK
    n_assign = n * TOP_K
    nb = -(-(n_assign + N_EXPERTS * (bm - 1)) // bm)
    eid, gates, rank, cnt = _router(x2d, w_hi, w_lo, bias, tri)
    counts = cnt[:, 0]
    padded = (counts + bm - 1) // bm * bm
    pend = jnp.cumsum(padded)
    pstart = (pend - padded).astype(I32)
    n_used = (pend[-1:] // bm).astype(I32)
    blk_start = jnp.arange(nb, dtype=I32) * bm
    blk_e = jnp.minimum(jnp.sum((pend[None, :] <= blk_start[:, None]).astype(I32), axis=1), N_EXPERTS - 1)
    rows, dest = _dispatch(pstart, eid, rank, x2d, jnp.zeros((nb * bm, d // 2), jnp.uint32))
    y_rows = _expert_ffn(blk_e, n_used, rows, w1, w3, w2, layer)
    return _combine_ln(dest, gates.T, x2d, y_rows, g, b, alpha)


def kernel(x, positions, ln_g, ln_b, mla_w_in, mla_q_norm, mla_w_uq, mla_kv_norm, mla_w_ukv, mla_w_o,
           pool_w, pool_b, pool_scale, moe_w_grp, moe_b_grp, moe_w_exp, moe_b_exp, moe_w1, moe_w3, moe_w2):
    batch, seq, d = x.shape
    depth = ln_g.shape[0]
    alpha = (2 * depth) ** 0.25
    tabs = _rope_tables(positions)
    t = ROUTE_TILE
    tri = (lax.broadcasted_iota(I32, (t, t), 0) < lax.broadcasted_iota(I32, (t, t), 1)).astype(BF16)
    x2d = x.reshape(batch * seq, d)
    vec = lambda v: v.reshape(1, d)
    for i in range(depth):
        j = i // 2
        if i % 2 == 0:
            w = _mla_weights(mla_w_in[j], mla_q_norm[j], mla_w_uq[j], mla_kv_norm[j], mla_w_ukv[j], mla_w_o[j])
            q_t, k, v_t = _mla_proj(x2d, tabs, w, batch, seq)
            o_t = _attention(q_t, k, v_t)
            x2d = _outproj_ln(o_t, x2d, w['w_o'], vec(ln_g[i, 0]), vec(ln_b[i, 0]), alpha)
        else:
            x2d = _pool_ln(x2d, seq, pool_w[j].astype(BF16), vec(pool_b[j]), vec(pool_scale[j]),
                           vec(ln_g[i, 0]), vec(ln_b[i, 0]), alpha)
        w_hi, w_lo, bias = _router_weights(moe_w_grp[i], moe_b_grp[i], moe_w_exp[i], moe_b_exp[i])
        x2d = _moe_ln(x2d, w_hi, w_lo, bias, tri, moe_w1, moe_w3, moe_w2, i,
                      vec(ln_g[i, 1]), vec(ln_b[i, 1]), alpha)
    return x2d.reshape(batch, seq, d)
```

```python
import functools
import math

import jax
import jax.numpy as jnp
from jax import lax
from jax.experimental import pallas as pl
from jax.experimental.pallas import tpu as pltpu

F32 = jnp.float32
BF16 = jnp.bfloat16
I32 = jnp.int32

N_HEADS = 16
QK_NOPE = 64
QK_ROPE = 32
HALF_ROPE = QK_ROPE // 2
V_DIM = 64
Q_LORA = 256
KV_LORA = 128
ROPE_THETA = 10000.0
POOL_WINDOWS = (2, 4, 8, 16)
N_GROUPS = 8
EXP_PER_GROUP = 8
N_EXPERTS = N_GROUPS * EXP_PER_GROUP
TOP_K = 2
LN_EPS = 1e-5
RMS_EPS = 1e-6

LANES = 128
HEAD_PAD = 128
V_ROWS = 80
POOL_HALO = 16

TOK_TILE = 512
ATT_TQ = 1024
ATT_TK = 512
ATT_HEADS_PER_STEP = 1
ROUTE_TILE = 512
DISPATCH_TILE = 512
COMBINE_TILE = 256
EXPERT_BLOCK = 512
FFN_CHUNK = 256

NEG_BIG = -1e30
_NT = (((1,), (1,)), ((), ()))
_TN = (((0,), (0,)), ((), ()))


def _layer_norm(y, g, b):
    mu = jnp.mean(y, axis=-1, keepdims=True)
    yc = y - mu
    var = jnp.mean(yc * yc, axis=-1, keepdims=True)
    return yc * lax.rsqrt(var + LN_EPS) * g + b


def _mla_proj_kernel(x_ref, cost_ref, sint_ref, cosp_ref, sinp_ref, wfm_ref, wtok_ref, wuq_ref,
                     wv_ref, wkn_ref, qn_ref, kvnc_ref, kvnr_ref, qt_ref, k_ref, vt_ref, *, q_scale):
    t = x_ref.shape[0]
    xb = x_ref[...].astype(BF16)
    lat_t = lax.dot_general(wfm_ref[...], xb, _NT, preferred_element_type=F32)
    cq = lat_t[0:Q_LORA]
    cq = cq * lax.rsqrt(jnp.mean(cq * cq, axis=0, keepdims=True) + RMS_EPS) * qn_ref[...]
    ckv = lat_t[Q_LORA:Q_LORA + KV_LORA]
    ckv = ckv * lax.rsqrt(jnp.mean(ckv * ckv, axis=0, keepdims=True) + RMS_EPS) * kvnc_ref[...]

    q_t = jnp.dot(wuq_ref[...], cq.astype(BF16), preferred_element_type=F32) * q_scale
    n_nope = N_HEADS * QK_NOPE
    n_half = N_HEADS * HALF_ROPE
    qn = q_t[0:n_nope].reshape(N_HEADS, QK_NOPE, t)
    x1 = q_t[n_nope:n_nope + n_half].reshape(N_HEADS, HALF_ROPE, t)
    x2 = q_t[n_nope + n_half:n_nope + 2 * n_half].reshape(N_HEADS, HALF_ROPE, t)
    c = cost_ref[...]
    s = sint_ref[...]
    qt_ref[0, :, 0:QK_NOPE, :] = qn.astype(BF16)
    qt_ref[0, :, QK_NOPE:QK_NOPE + HALF_ROPE, :] = (x1 * c - x2 * s).astype(BF16)
    qt_ref[0, :, QK_NOPE + HALF_ROPE:QK_NOPE + QK_ROPE, :] = (x2 * c + x1 * s).astype(BF16)
    qt_ref[0, :, QK_NOPE + QK_ROPE:HEAD_PAD, :] = jnp.zeros(
        (N_HEADS, HEAD_PAD - QK_NOPE - QK_ROPE, t), BF16)

    v_t = jnp.dot(wv_ref[...], ckv.astype(BF16), preferred_element_type=F32)
    vt_ref[0, :, 0:V_DIM, :] = v_t.reshape(N_HEADS, V_DIM, t).astype(BF16)
    vt_ref[0, :, V_DIM:V_ROWS, :] = jnp.ones((N_HEADS, V_ROWS - V_DIM, t), BF16)

    lat = jnp.dot(xb, wtok_ref[...], preferred_element_type=F32)
    ckv_tok = lat[:, 0:KV_LORA]
    ckv_tok = ckv_tok * lax.rsqrt(jnp.mean(ckv_tok * ckv_tok, axis=-1, keepdims=True) + RMS_EPS) * kvnr_ref[...]
    k_rope = lat[:, KV_LORA:KV_LORA + LANES] * cosp_ref[...] + lat[:, KV_LORA + LANES:] * sinp_ref[...]
    kn = jnp.dot(ckv_tok.astype(BF16), wkn_ref[...], preferred_element_type=F32)
    for h in range(N_HEADS):
        k_ref[0, h] = (kn[:, h * HEAD_PAD:(h + 1) * HEAD_PAD] + k_rope).astype(BF16)


def _mla_proj(x2d, tabs, w, batch, seq):
    n, d = x2d.shape
    t = TOK_TILE
    tiles_per_seq = seq // t
    q_scale = (QK_NOPE + QK_ROPE) ** -0.5 * math.log2(math.e)
    full = lambda a: pl.BlockSpec(a.shape, lambda i: (0,) * a.ndim)
    bs_idx = lambda i: (i // tiles_per_seq, 0, 0, i % tiles_per_seq)
    return pl.pallas_call(
        functools.partial(_mla_proj_kernel, q_scale=q_scale),
        grid=(n // t,),
        in_specs=[
            pl.BlockSpec((t, d), lambda i: (i, 0)),
            pl.BlockSpec((1, HALF_ROPE, t), lambda i: (i // tiles_per_seq, 0, i % tiles_per_seq)),
            pl.BlockSpec((1, HALF_ROPE, t), lambda i: (i // tiles_per_seq, 0, i % tiles_per_seq)),
            pl.BlockSpec((t, LANES), lambda i: (i, 0)),
            pl.BlockSpec((t, LANES), lambda i: (i, 0)),
            full(w['w_fm']), full(w['w_tok']), full(w['w_uq_t']), full(w['w_v_t']), full(w['w_kn']),
            full(w['qn_col']), full(w['kvn_col']), full(w['kvn_row']),
        ],
        out_specs=[
            pl.BlockSpec((1, N_HEADS, HEAD_PAD, t), bs_idx),
            pl.BlockSpec((1, N_HEADS, t, HEAD_PAD), lambda i: (i // tiles_per_seq, 0, i % tiles_per_seq, 0)),
            pl.BlockSpec((1, N_HEADS, V_ROWS, t), bs_idx),
        ],
        out_shape=[
            jax.ShapeDtypeStruct((batch, N_HEADS, HEAD_PAD, seq), BF16),
            jax.ShapeDtypeStruct((batch, N_HEADS, seq, HEAD_PAD), BF16),
            jax.ShapeDtypeStruct((batch, N_HEADS, V_ROWS, seq), BF16),
        ],
        compiler_params=pltpu.CompilerParams(dimension_semantics=("arbitrary",),
                                             vmem_limit_bytes=56 * 1024 * 1024),
        name="mla_proj",
    )(x2d, tabs['cos_t'], tabs['sin_t'], tabs['cos_pad'], tabs['sin_pad'],
      w['w_fm'], w['w_tok'], w['w_uq_t'], w['w_v_t'], w['w_kn'], w['qn_col'], w['kvn_col'], w['kvn_row'])


def _attn_kernel(qt_ref, k_ref, vt_ref, o_ref, s_scr, p_scr):
    n_h = qt_ref.shape[1]
    seq = k_ref.shape[2]
    tq, tk = ATT_TQ, ATT_TK
    n_q = seq // tq
    heads = range(n_h)

    def load_q(q0):
        return [qt_ref[0, h, :, pl.ds(q0, tq)] for h in heads]

    def scores(h, q_t, k0, slot):
        kb = k_ref[0, h, pl.ds(k0, tk), :]
        s = jnp.dot(kb, q_t, preferred_element_type=F32)
        s_scr[h, slot] = s
        return jnp.max(s, axis=0, keepdims=True)

    def softmax(h, slot, m, bmax, diag_off):
        s = s_scr[h, slot]
        if diag_off is not None:
            row = lax.broadcasted_iota(I32, (tk, tq), 0) + diag_off
            col = lax.broadcasted_iota(I32, (tk, tq), 1)
            s = jnp.where(row <= col, s, NEG_BIG)
            bmax = jnp.max(s, axis=0, keepdims=True)
        m_new = jnp.maximum(m, bmax)
        p_scr[h, slot] = jnp.exp2(s - m_new).astype(BF16)
        return m_new, jnp.exp2(m - m_new)

    def values(h, k0, slot, alpha, acc):
        vb = vt_ref[0, h, :, pl.ds(k0, tk)]
        return alpha * acc + jnp.dot(vb, p_scr[h, slot], preferred_element_type=F32)

    def trip(t, carry, q_ts, q_next):
        last = q_next is not None
        m, acc, a_pend, bmax0 = [list(c) for c in carry]
        ka = pl.multiple_of(t * tq, tq)
        k_prev = pl.multiple_of(jnp.maximum(ka - tk, 0), tk)
        a0, bmax1 = [None] * n_h, [None] * n_h
        for h in heads:
            bmax1[h] = scores(h, q_ts[h], ka + tk, 1)
        for h in heads:
            m[h], a0[h] = softmax(h, 0, m[h], bmax0[h], 0 if last else None)
        for h in heads:
            acc[h] = values(h, k_prev, 1, a_pend[h], acc[h])
        for h in heads:
            bmax0[h] = scores(h, q_next[h], 0, 0) if last else scores(h, q_ts[h], ka + tq, 0)
        for h in heads:
            m[h], a_pend[h] = softmax(h, 1, m[h], bmax1[h], tk if last else None)
        for h in heads:
            acc[h] = values(h, ka, 0, a0[h], acc[h])
        return tuple(m), tuple(acc), tuple(a_pend), tuple(bmax0)

    def q_tile(qi, bmax0):
        q0 = pl.multiple_of(qi * tq, tq)
        q_ts = load_q(q0)
        q_next = load_q(pl.multiple_of(jnp.minimum(q0 + tq, seq - tq), tq))
        for h in heads:
            p_scr[h, 1] = jnp.zeros((tk, tq), BF16)
        init = (tuple(jnp.full((1, tq), NEG_BIG, F32) for _ in heads),
                tuple(jnp.zeros((V_ROWS, tq), F32) for _ in heads),
                tuple(jnp.ones((1, tq), F32) for _ in heads),
                bmax0)
        res = lax.fori_loop(0, qi, lambda t, c: trip(t, c, q_ts, None), init)
        m, acc, a_pend, bmax0 = trip(qi, res, q_ts, q_next)
        for h in heads:
            out = values(h, q0 + tk, 1, a_pend[h], acc[h])
            o_ref[0, h * V_DIM:(h + 1) * V_DIM, pl.ds(q0, tq)] = (
                out[0:V_DIM] / out[V_DIM:V_DIM + 1]).astype(BF16)
        return bmax0

    q_first = load_q(0)
    lax.fori_loop(0, n_q, q_tile, tuple(scores(h, q_first[h], 0, 0) for h in heads))


def _attention(q_t, k, v_t):
    batch, heads, _, seq = q_t.shape
    g = ATT_HEADS_PER_STEP
    assert ATT_TQ == 2 * ATT_TK and seq % ATT_TQ == 0 and heads % g == 0
    return pl.pallas_call(
        _attn_kernel,
        grid=(batch, heads // g),
        in_specs=[
            pl.BlockSpec((1, g, HEAD_PAD, seq), lambda b, h: (b, h, 0, 0)),
            pl.BlockSpec((1, g, seq, HEAD_PAD), lambda b, h: (b, h, 0, 0)),
            pl.BlockSpec((1, g, V_ROWS, seq), lambda b, h: (b, h, 0, 0)),
        ],
        out_specs=pl.BlockSpec((1, g * V_DIM, seq), lambda b, h: (b, h, 0)),
        out_shape=jax.ShapeDtypeStruct((batch, heads * V_DIM, seq), BF16),
        scratch_shapes=[pltpu.VMEM((g, 2, ATT_TK, ATT_TQ), F32), pltpu.VMEM((g, 2, ATT_TK, ATT_TQ), BF16)],
        compiler_params=pltpu.CompilerParams(dimension_semantics=("arbitrary", "arbitrary"),
                                             vmem_limit_bytes=56 * 1024 * 1024),
        name="mla_attention",
    )(q_t, k, v_t)


def _outproj_ln_kernel(ot_ref, x_ref, wo_ref, g_ref, b_ref, out_ref, *, alpha):
    f = lax.dot_general(ot_ref[0], wo_ref[...], _TN, preferred_element_type=F32)
    out_ref[...] = _layer_norm(alpha * x_ref[...] + f, g_ref[...], b_ref[...])


def _outproj_ln(o_t, x2d, w_o, g, b, alpha):
    n, d = x2d.shape
    batch, hd, seq = o_t.shape
    t = TOK_TILE
    tiles_per_seq = seq // t
    return pl.pallas_call(
        functools.partial(_outproj_ln_kernel, alpha=alpha),
        grid=(n // t,),
        in_specs=[
            pl.BlockSpec((1, hd, t), lambda i: (i // tiles_per_seq, 0, i % tiles_per_seq)),
            pl.BlockSpec((t, d), lambda i: (i, 0)),
            pl.BlockSpec((hd, d), lambda i: (0, 0)),
            pl.BlockSpec((1, d), lambda i: (0, 0)),
            pl.BlockSpec((1, d), lambda i: (0, 0)),
        ],
        out_specs=pl.BlockSpec((t, d), lambda i: (i, 0)),
        out_shape=jax.ShapeDtypeStruct((n, d), F32),
        compiler_params=pltpu.CompilerParams(dimension_semantics=("arbitrary",)),
        name="mla_outproj_ln",
    )(o_t, x2d, w_o, g, b)


def _pool_ln_kernel(x_ref, halo_ref, w_ref, pb_ref, ps_ref, g_ref, b_ref, out_ref, *, alpha, tiles_per_seq):
    t, d = x_ref.shape
    gd = d // len(POOL_WINDOWS)
    i = pl.program_id(0)
    tile_in_seq = i % tiles_per_seq
    x = x_ref[...]
    halo = jnp.where(tile_in_seq == 0, 0.0, halo_ref[...])
    ext = jnp.concatenate([halo, x], axis=0)
    pos = tile_in_seq * t + lax.broadcasted_iota(I32, (t, 1), 0)
    ys = []
    for gi, win in enumerate(POOL_WINDOWS):
        e = ext[:, gi * gd:(gi + 1) * gd]
        sh = 1
        while sh < win:
            e = e + pltpu.roll(e, sh, axis=0)
            sh *= 2
        cnt = jnp.minimum(pos + 1, win).astype(F32)
        pooled = e[POOL_HALO:] / cnt - x[:, gi * gd:(gi + 1) * gd]
        ys.append(jnp.dot(pooled.astype(BF16), w_ref[gi], preferred_element_type=F32))
    y = (jnp.concatenate(ys, axis=1) + pb_ref[...]) * ps_ref[...]
    out_ref[...] = _layer_norm(alpha * x + y, g_ref[...], b_ref[...])


def _pool_ln(x2d, seq, w, pb, ps, g, b, alpha):
    n, d = x2d.shape
    t = TOK_TILE
    tiles_per_seq = seq // t
    halo_blocks = t // POOL_HALO
    vec = pl.BlockSpec((1, d), lambda i: (0, 0))
    return pl.pallas_call(
        functools.partial(_pool_ln_kernel, alpha=alpha, tiles_per_seq=tiles_per_seq),
        grid=(n // t,),
        in_specs=[
            pl.BlockSpec((t, d), lambda i: (i, 0)),
            pl.BlockSpec((POOL_HALO, d), lambda i: (jnp.maximum(i * halo_blocks - 1, 0), 0)),
            pl.BlockSpec(w.shape, lambda i: (0, 0, 0)),
            vec, vec, vec, vec,
        ],
        out_specs=pl.BlockSpec((t, d), lambda i: (i, 0)),
        out_shape=jax.ShapeDtypeStruct((n, d), F32),
        compiler_params=pltpu.CompilerParams(dimension_semantics=("arbitrary",)),
        name="pool_ln",
    )(x2d, x2d, w, pb, ps, g, b)


def _router_kernel(x_ref, whi_ref, wlo_ref, bias_ref, tri_ref, eid_ref, gate_ref, rank_ref, cnt_ref, run_ref):
    t = x_ref.shape[0]

    @pl.when(pl.program_id(0) == 0)
    def _():
        run_ref[...] = jnp.zeros_like(run_ref)

    x = x_ref[...]
    xh = x.astype(BF16)
    xl = (x - xh.astype(F32)).astype(BF16)
    lg = (lax.dot_general(whi_ref[...], xh, _NT, preferred_element_type=F32)
          + lax.dot_general(whi_ref[...], xl, _NT, preferred_element_type=F32)
          + lax.dot_general(wlo_ref[...], xh, _NT, preferred_element_type=F32)
          + bias_ref[...])
    ridx = lax.broadcasted_iota(I32, (N_GROUPS, t), 0)

    g = lg[0:N_GROUPS]
    gmax = jnp.max(g, axis=0, keepdims=True)
    g_sel = jnp.min(jnp.where(g == gmax, ridx, N_GROUPS), axis=0, keepdims=True)
    g_gate = 1.0 / jnp.sum(jnp.exp(g - gmax), axis=0, keepdims=True)

    e_all = lg[N_GROUPS:N_GROUPS + N_EXPERTS]
    e = jnp.zeros((EXP_PER_GROUP, t), F32)
    for gi in range(N_GROUPS):
        e = e + jnp.where(g_sel == gi, e_all[gi * EXP_PER_GROUP:(gi + 1) * EXP_PER_GROUP], 0.0)
    pe = jnp.exp(e - jnp.max(e, axis=0, keepdims=True))
    p1 = jnp.max(pe, axis=0, keepdims=True)
    i1 = jnp.min(jnp.where(pe == p1, ridx, EXP_PER_GROUP), axis=0, keepdims=True)
    pe2 = jnp.where(ridx == i1, -1.0, pe)
    p2 = jnp.max(pe2, axis=0, keepdims=True)
    i2 = jnp.min(jnp.where(pe2 == p2, ridx, EXP_PER_GROUP), axis=0, keepdims=True)
    denom = p1 + p2
    e1 = g_sel * EXP_PER_GROUP + i1
    e2 = g_sel * EXP_PER_GROUP + i2
    eid_ref[0:1, :] = e1
    eid_ref[1:2, :] = e2
    gate_ref[0:1, :] = g_gate * p1 / denom
    gate_ref[1:2, :] = g_gate * p2 / denom

    eidx = lax.broadcasted_iota(I32, (N_EXPERTS, t), 0)
    oh1 = eidx == e1
    oh2 = eidx == e2
    oh = jnp.where(oh1 | oh2, 1.0, 0.0)
    earlier = jnp.dot(oh.astype(BF16), tri_ref[...], preferred_element_type=F32)
    tot = earlier + run_ref[:, 0:1]
    rank_ref[0:1, :] = jnp.sum(jnp.where(oh1, tot, 0.0), axis=0, keepdims=True).astype(I32)
    rank_ref[1:2, :] = jnp.sum(jnp.where(oh2, tot, 0.0), axis=0, keepdims=True).astype(I32)
    run_ref[...] = run_ref[...] + jnp.sum(oh, axis=1, keepdims=True)
    cnt_ref[...] = run_ref[...].astype(I32)


def _router(x2d, w_hi, w_lo, bias, tri):
    n, d = x2d.shape
    t = ROUTE_TILE
    tok = lambda dt: jax.ShapeDtypeStruct((TOP_K, n), dt)
    tok_spec = pl.BlockSpec((TOP_K, t), lambda i: (0, i))
    return pl.pallas_call(
        _router_kernel,
        grid=(n // t,),
        in_specs=[
            pl.BlockSpec((t, d), lambda i: (i, 0)),
            pl.BlockSpec(w_hi.shape, lambda i: (0, 0)),
            pl.BlockSpec(w_lo.shape, lambda i: (0, 0)),
            pl.BlockSpec(bias.shape, lambda i: (0, 0)),
            pl.BlockSpec(tri.shape, lambda i: (0, 0)),
        ],
        out_specs=[tok_spec, tok_spec, tok_spec, pl.BlockSpec((N_EXPERTS, LANES), lambda i: (0, 0))],
        out_shape=[tok(I32), tok(F32), tok(I32), jax.ShapeDtypeStruct((N_EXPERTS, LANES), I32)],
        scratch_shapes=[pltpu.VMEM((N_EXPERTS, LANES), F32)],
        compiler_params=pltpu.CompilerParams(dimension_semantics=("arbitrary",)),
        name="moe_router",
    )(x2d, w_hi, w_lo, bias, tri)


ROWS_PER_TRIP = 8


def _pack_halves(y):
    half = y.shape[1] // 2
    bits = lax.bitcast_convert_type(y.astype(BF16).astype(F32), jnp.uint32)
    return bits[:, :half] | (bits[:, half:] >> 16)


def _unpack_halves(p):
    hi = lax.bitcast_convert_type(p & jnp.uint32(0xFFFF0000), F32)
    lo = lax.bitcast_convert_type(p << 16, F32)
    return hi, lo


def _for_row_groups(n_rows, body):
    def trip(g, c):
        base = g * ROWS_PER_TRIP
        for u in range(ROWS_PER_TRIP):
            body(g, u, base + u)
        return c
    lax.fori_loop(0, n_rows // ROWS_PER_TRIP, trip, 0)


def _staged_row(buf, g, u):
    return buf.at[g, pl.ds(u, 1)]


def _hbm_row(arr, row):
    return arr.at[pl.ds(row, 1)]


def _slots_kernel(pstart_ref, eid_ref, rank_ref, dest_ref):
    eid = eid_ref[...]

    def add_expert(e, acc):
        return acc + jnp.where(eid == e, pstart_ref[e], 0)

    dest_ref[...] = lax.fori_loop(0, N_EXPERTS, add_expert, rank_ref[...])


def _slots(pstart, eid, rank):
    full = pl.BlockSpec(eid.shape, lambda i, ps: (0, 0))
    return pl.pallas_call(
        _slots_kernel,
        grid_spec=pltpu.PrefetchScalarGridSpec(num_scalar_prefetch=1, grid=(1,),
                                               in_specs=[full, full], out_specs=full),
        out_shape=jax.ShapeDtypeStruct(eid.shape, I32),
        name="moe_slots",
    )(pstart, eid, rank)


def _dispatch_kernel(d0_ref, d1_ref, x_ref, rows_in_hbm, rows_hbm, xp_scr, sem):
    del rows_in_hbm
    t = x_ref.shape[0]
    xp_scr[...] = _pack_halves(x_ref[...]).reshape(xp_scr.shape)

    def issue(g, u, r):
        for k, d_ref in enumerate((d0_ref, d1_ref)):
            pltpu.make_async_copy(_staged_row(xp_scr, g, u), _hbm_row(rows_hbm, d_ref[r]), sem).start(priority=k)

    def drain(g, u, r):
        for _ in range(TOP_K):
            pltpu.make_async_copy(_staged_row(xp_scr, 0, 0), _hbm_row(rows_hbm, 0), sem).wait()

    _for_row_groups(t, issue)
    _for_row_groups(t, drain)


def _dispatch(dest, x2d, rows_init):
    n, d = x2d.shape
    t = DISPATCH_TILE
    tok_spec = pl.BlockSpec((t,), lambda i: (i,), memory_space=pltpu.SMEM)
    return pl.pallas_call(
        _dispatch_kernel,
        grid=(n // t,),
        in_specs=[
            tok_spec, tok_spec,
            pl.BlockSpec((t, d), lambda i: (i, 0)),
            pl.BlockSpec(memory_space=pl.ANY),
        ],
        out_specs=pl.BlockSpec(memory_space=pl.ANY),
        out_shape=jax.ShapeDtypeStruct(rows_init.shape, rows_init.dtype),
        scratch_shapes=[pltpu.VMEM((t // ROWS_PER_TRIP, ROWS_PER_TRIP, d // 2), jnp.uint32),
                        pltpu.SemaphoreType.DMA(())],
        input_output_aliases={3: 0},
        compiler_params=pltpu.CompilerParams(dimension_semantics=("arbitrary",)),
        name="moe_dispatch",
    )(dest[0], dest[1], x2d, rows_init)


def _ffn_kernel(be_ref, nu_ref, rows_ref, w1_ref, w3_ref, w2_ref, y_ref, w1b, w3b, w2b):
    b = pl.program_id(0)
    half = rows_ref.shape[1]

    @pl.when(b < nu_ref[0])
    def _():
        prev = be_ref[jnp.maximum(b - 1, 0)]

        @pl.when((b == 0) | (be_ref[b] != prev))
        def _():
            w1b[...] = w1_ref[0, 0].astype(BF16)
            w3b[...] = w3_ref[0, 0].astype(BF16)
            w2b[...] = w2_ref[0, 0].astype(BF16)

        for c in range(rows_ref.shape[0] // FFN_CHUNK):
            rows = pl.ds(c * FFN_CHUNK, FFN_CHUNK)
            x_hi, x_lo = _unpack_halves(rows_ref[rows, :])
            x_hi, x_lo = x_hi.astype(BF16), x_lo.astype(BF16)

            def up(w):
                return (jnp.dot(x_hi, w[0:half, :], preferred_element_type=F32)
                        + jnp.dot(x_lo, w[half:2 * half, :], preferred_element_type=F32))

            h1 = up(w1b)
            hb = h1 * jax.nn.sigmoid(h1) * up(w3b)
            y_ref[rows, :] = _pack_halves(jnp.dot(hb.astype(BF16), w2b[...], preferred_element_type=F32))


def _expert_ffn(blk_e, n_used, rows, w1, w3, w2, layer):
    r, half = rows.shape
    d, de = w1.shape[2], w1.shape[3]
    bm = EXPERT_BLOCK
    nb = r // bm
    clamp = lambda b, nu: jnp.minimum(b, nu[0] - 1)
    row_spec = pl.BlockSpec((bm, half), lambda b, be, nu: (clamp(b, nu), 0))
    w_idx = lambda b, be, nu: (layer, be[clamp(b, nu)], 0, 0)
    return pl.pallas_call(
        _ffn_kernel,
        grid_spec=pltpu.PrefetchScalarGridSpec(
            num_scalar_prefetch=2,
            grid=(nb,),
            in_specs=[
                row_spec,
                pl.BlockSpec((1, 1, d, de), w_idx),
                pl.BlockSpec((1, 1, d, de), w_idx),
                pl.BlockSpec((1, 1, de, d), w_idx),
            ],
            out_specs=row_spec,
            scratch_shapes=[pltpu.VMEM((d, de), BF16), pltpu.VMEM((d, de), BF16), pltpu.VMEM((de, d), BF16)],
        ),
        out_shape=jax.ShapeDtypeStruct((r, half), jnp.uint32),
        compiler_params=pltpu.CompilerParams(dimension_semantics=("arbitrary",)),
        name="moe_expert_ffn",
    )(blk_e, n_used, rows, w1, w3, w2)


def _combine_ln_kernel(d0_ref, d1_ref, n0_ref, n1_ref, gate_ref, x_ref, y_hbm, g_ref, b_ref, out_ref,
                       ybuf, sems, *, alpha):
    t = COMBINE_TILE
    i = pl.program_id(0)

    def request_row(slot, d_refs, row, g, u):
        for k, d_ref in enumerate(d_refs):
            pltpu.make_async_copy(_hbm_row(y_hbm, d_ref[row]), _staged_row(ybuf.at[slot, k], g, u),
                                  sems.at[slot]).start(priority=k)

    def wait_tile(slot):
        def wait(g, u, r):
            for k in range(TOP_K):
                pltpu.make_async_copy(_hbm_row(y_hbm, 0), _staged_row(ybuf.at[slot, k], 0, 0), sems.at[slot]).wait()
        _for_row_groups(t, wait)

    def request_tile(slot, d_refs, first_row):
        _for_row_groups(t, lambda g, u, r: request_row(slot, d_refs, first_row + r, g, u))

    def reduce_tile(slot, first_row):
        wait_tile(slot)
        rows = pl.ds(first_row, t)
        gate = gate_ref[rows, :]
        hi0, lo0 = _unpack_halves(ybuf[slot, 0].reshape(t, -1))
        hi1, lo1 = _unpack_halves(ybuf[slot, 1].reshape(t, -1))
        g0, g1 = gate[:, 0:1], gate[:, 1:2]
        f = jnp.concatenate([g0 * hi0 + g1 * hi1, g0 * lo0 + g1 * lo1], axis=1)
        out_ref[rows, :] = _layer_norm(alpha * x_ref[rows, :] + f, g_ref[...], b_ref[...])

    @pl.when(i == 0)
    def _():
        request_tile(0, (d0_ref, d1_ref), 0)

    request_tile(1, (d0_ref, d1_ref), t)
    reduce_tile(0, 0)

    @pl.when(i + 1 < pl.num_programs(0))
    def _():
        request_tile(0, (n0_ref, n1_ref), 0)

    reduce_tile(1, t)


def _combine_ln(dest, gates_tok, x2d, y_rows, g, b, alpha):
    n, d = x2d.shape
    t = COMBINE_TILE
    n_tiles = n // t
    vec = pl.BlockSpec((1, d), lambda i: (0, 0))
    pair_spec = pl.BlockSpec((2 * t,), lambda i: (i,), memory_space=pltpu.SMEM)
    next_spec = pl.BlockSpec((t,), lambda i: (jnp.minimum(2 * i + 2, n_tiles - 1),), memory_space=pltpu.SMEM)
    return pl.pallas_call(
        functools.partial(_combine_ln_kernel, alpha=alpha),
        grid=(n_tiles // 2,),
        in_specs=[
            pair_spec, pair_spec, next_spec, next_spec,
            pl.BlockSpec((2 * t, TOP_K), lambda i: (i, 0)),
            pl.BlockSpec((2 * t, d), lambda i: (i, 0)),
            pl.BlockSpec(memory_space=pl.ANY),
            vec, vec,
        ],
        out_specs=pl.BlockSpec((2 * t, d), lambda i: (i, 0)),
        out_shape=jax.ShapeDtypeStruct((n, d), F32),
        scratch_shapes=[pltpu.VMEM((2, TOP_K, t // ROWS_PER_TRIP, ROWS_PER_TRIP, d // 2), jnp.uint32),
                        pltpu.SemaphoreType.DMA((2,))],
        compiler_params=pltpu.CompilerParams(dimension_semantics=("arbitrary",)),
        name="moe_combine_ln",
    )(dest[0], dest[1], dest[0], dest[1], gates_tok, x2d, y_rows, g, b)


def _rope_tables(positions):
    inv_freq = ROPE_THETA ** (-jnp.arange(0, QK_ROPE, 2, dtype=F32) / QK_ROPE)
    ang = positions.astype(F32)[..., None] * inv_freq
    cos, sin = jnp.cos(ang), jnp.sin(ang)
    b, s, _ = cos.shape
    zeros = jnp.zeros((b * s, QK_NOPE), F32)
    tail = jnp.zeros((b * s, LANES - QK_NOPE - QK_ROPE), F32)
    c2, s2 = cos.reshape(b * s, -1), sin.reshape(b * s, -1)
    return {
        'cos_t': jnp.swapaxes(cos, 1, 2), 'sin_t': jnp.swapaxes(sin, 1, 2),
        'cos_pad': jnp.concatenate([zeros, c2, c2, tail], axis=1),
        'sin_pad': jnp.concatenate([zeros, s2, s2, tail], axis=1),
    }


def _mla_weights(w_in, q_norm, w_uq, kv_norm, w_ukv, w_o):
    d = w_in.shape[0]
    lat_q_kv = Q_LORA + KV_LORA
    r1 = w_in[:, lat_q_kv:lat_q_kv + HALF_ROPE]
    r2 = w_in[:, lat_q_kv + HALF_ROPE:lat_q_kv + QK_ROPE]
    z_lo = jnp.zeros((d, QK_NOPE), F32)
    z_hi = jnp.zeros((d, LANES - QK_NOPE - QK_ROPE), F32)
    w_tok = jnp.concatenate([w_in[:, Q_LORA:lat_q_kv],
                             z_lo, r1, r2, z_hi,
                             z_lo, -r2, r1, z_hi],
                            axis=1)
    uq = w_uq.reshape(Q_LORA, N_HEADS, QK_NOPE + QK_ROPE)
    w_uq_t = jnp.concatenate([uq[:, :, :QK_NOPE].reshape(Q_LORA, -1),
                              uq[:, :, QK_NOPE:QK_NOPE + HALF_ROPE].reshape(Q_LORA, -1),
                              uq[:, :, QK_NOPE + HALF_ROPE:].reshape(Q_LORA, -1)], axis=1).T
    ukv = w_ukv.reshape(KV_LORA, N_HEADS, QK_NOPE + V_DIM)
    w_kn = jnp.concatenate([ukv[:, :, :QK_NOPE], jnp.zeros((KV_LORA, N_HEADS, HEAD_PAD - QK_NOPE), F32)],
                           axis=2).reshape(KV_LORA, N_HEADS * HEAD_PAD)
    w_v_t = ukv[:, :, QK_NOPE:].reshape(KV_LORA, N_HEADS * V_DIM).T
    return {
        'w_fm': w_in[:, :lat_q_kv].T.astype(BF16), 'w_tok': w_tok.astype(BF16),
        'w_uq_t': w_uq_t.astype(BF16), 'w_v_t': w_v_t.astype(BF16), 'w_kn': w_kn.astype(BF16),
        'qn_col': q_norm.reshape(-1, 1), 'kvn_col': kv_norm.reshape(-1, 1), 'kvn_row': kv_norm.reshape(1, -1),
        'w_o': w_o.astype(BF16),
    }


def _router_weights(w_grp, b_grp, w_exp, b_exp):
    d = w_grp.shape[0]
    pad = LANES - N_GROUPS - N_EXPERTS
    w = jnp.concatenate([w_grp, w_exp, jnp.zeros((d, pad), F32)], axis=1).T
    w_hi = w.astype(BF16)
    w_lo = (w - w_hi.astype(F32)).astype(BF16)
    bias = jnp.concatenate([b_grp, b_exp, jnp.zeros((pad,), F32)]).reshape(-1, 1)
    return w_hi, w_lo, bias


def _moe_ln(x2d, w_hi, w_lo, bias, tri, w1, w3, w2, layer, g, b, alpha):
    n, d = x2d.shape
    bm = EXPERT_BLOCK
    n_assign = n * TOP_K
    nb = -(-(n_assign + N_EXPERTS * (bm - 1)) // bm)
    eid, gates, rank, cnt = _router(x2d, w_hi, w_lo, bias, tri)
    counts = cnt[:, 0]
    padded = (counts + bm - 1) // bm * bm
    pend = jnp.cumsum(padded)
    pstart = (pend - padded).astype(I32)
    n_used = (pend[-1:] // bm).astype(I32)
    blk_start = jnp.arange(nb, dtype=I32) * bm
    blk_e = jnp.minimum(jnp.sum((pend[None, :] <= blk_start[:, None]).astype(I32), axis=1), N_EXPERTS - 1)
    dest = _slots(pstart, eid, rank)
    rows = _dispatch(dest, x2d, jnp.zeros((nb * bm, d // 2), jnp.uint32))
    y_rows = _expert_ffn(blk_e, n_used, rows, w1, w3, w2, layer)
    return _combine_ln(dest, gates.T, x2d, y_rows, g, b, alpha)


def kernel(x, positions, ln_g, ln_b, mla_w_in, mla_q_norm, mla_w_uq, mla_kv_norm, mla_w_ukv, mla_w_o,
           pool_w, pool_b, pool_scale, moe_w_grp, moe_b_grp, moe_w_exp, moe_b_exp, moe_w1, moe_w3, moe_w2):
    batch, seq, d = x.shape
    depth = ln_g.shape[0]
    alpha = (2 * depth) ** 0.25
    tabs = _rope_tables(positions)
    t = ROUTE_TILE
    tri = (lax.broadcasted_iota(I32, (t, t), 0) < lax.broadcasted_iota(I32, (t, t), 1)).astype(BF16)
    x2d = x.reshape(batch * seq, d)
    vec = lambda v: v.reshape(1, d)
    for i in range(depth):
        j = i // 2
        if i % 2 == 0:
            w = _mla_weights(mla_w_in[j], mla_q_norm[j], mla_w_uq[j], mla_kv_norm[j], mla_w_ukv[j], mla_w_o[j])
            q_t, k, v_t = _mla_proj(x2d, tabs, w, batch, seq)
            o_t = _attention(q_t, k, v_t)
            x2d = _outproj_ln(o_t, x2d, w['w_o'], vec(ln_g[i, 0]), vec(ln_b[i, 0]), alpha)
        else:
            x2d = _pool_ln(x2d, seq, pool_w[j].astype(BF16), vec(pool_b[j]), vec(pool_scale[j]),
                           vec(ln_g[i, 0]), vec(ln_b[i, 0]), alpha)
        w_hi, w_lo, bias = _router_weights(moe_w_grp[i], moe_b_grp[i], moe_w_exp[i], moe_b_exp[i])
        x2d = _moe_ln(x2d, w_hi, w_lo, bias, tri, moe_w1, moe_w3, moe_w2, i,
                      vec(ln_g[i, 1]), vec(ln_b[i, 1]), alpha)
    return x2d.reshape(batch, seq, d)
```

```python
import functools
import math

import jax
import jax.numpy as jnp
from jax import lax
from jax.experimental import pallas as pl
from jax.experimental.pallas import tpu as pltpu

F32 = jnp.float32
BF16 = jnp.bfloat16
I32 = jnp.int32

N_HEADS = 16
QK_NOPE = 64
QK_ROPE = 32
HALF_ROPE = QK_ROPE // 2
V_DIM = 64
Q_LORA = 256
KV_LORA = 128
ROPE_THETA = 10000.0
POOL_WINDOWS = (2, 4, 8, 16)
N_GROUPS = 8
EXP_PER_GROUP = 8
N_EXPERTS = N_GROUPS * EXP_PER_GROUP
TOP_K = 2
LN_EPS = 1e-5
RMS_EPS = 1e-6

LANES = 128
HEAD_PAD = 128
V_ROWS = 80
POOL_HALO = 16

TOK_TILE = 512
ATT_TQ = 1024
ATT_TK = 512
ATT_HEADS_PER_STEP = 1
ROUTE_TILE = 512
DISPATCH_TILE = 512
COMBINE_TILE = 256
EXPERT_BLOCK = 512
FFN_CHUNK = 256

NEG_BIG = -1e30
_NT = (((1,), (1,)), ((), ()))
_TN = (((0,), (0,)), ((), ()))


def _layer_norm(y, g, b):
    mu = jnp.mean(y, axis=-1, keepdims=True)
    yc = y - mu
    var = jnp.mean(yc * yc, axis=-1, keepdims=True)
    return yc * lax.rsqrt(var + LN_EPS) * g + b


def _mla_proj_kernel(x_ref, cost_ref, sint_ref, cosp_ref, sinp_ref, wfm_ref, wtok_ref, wuq_ref,
                     wv_ref, wkn_ref, qn_ref, kvnc_ref, kvnr_ref, qt_ref, k_ref, vt_ref, *, q_scale):
    t = x_ref.shape[0]
    xb = x_ref[...].astype(BF16)
    lat_t = lax.dot_general(wfm_ref[...], xb, _NT, preferred_element_type=F32)
    cq = lat_t[0:Q_LORA]
    cq = cq * lax.rsqrt(jnp.mean(cq * cq, axis=0, keepdims=True) + RMS_EPS) * qn_ref[...]
    ckv = lat_t[Q_LORA:Q_LORA + KV_LORA]
    ckv = ckv * lax.rsqrt(jnp.mean(ckv * ckv, axis=0, keepdims=True) + RMS_EPS) * kvnc_ref[...]

    q_t = jnp.dot(wuq_ref[...], cq.astype(BF16), preferred_element_type=F32) * q_scale
    n_nope = N_HEADS * QK_NOPE
    n_half = N_HEADS * HALF_ROPE
    qn = q_t[0:n_nope].reshape(N_HEADS, QK_NOPE, t)
    x1 = q_t[n_nope:n_nope + n_half].reshape(N_HEADS, HALF_ROPE, t)
    x2 = q_t[n_nope + n_half:n_nope + 2 * n_half].reshape(N_HEADS, HALF_ROPE, t)
    c = cost_ref[...]
    s = sint_ref[...]
    qt_ref[0, :, 0:QK_NOPE, :] = qn.astype(BF16)
    qt_ref[0, :, QK_NOPE:QK_NOPE + HALF_ROPE, :] = (x1 * c - x2 * s).astype(BF16)
    qt_ref[0, :, QK_NOPE + HALF_ROPE:QK_NOPE + QK_ROPE, :] = (x2 * c + x1 * s).astype(BF16)
    qt_ref[0, :, QK_NOPE + QK_ROPE:HEAD_PAD, :] = jnp.zeros(
        (N_HEADS, HEAD_PAD - QK_NOPE - QK_ROPE, t), BF16)

    v_t = jnp.dot(wv_ref[...], ckv.astype(BF16), preferred_element_type=F32)
    vt_ref[0, :, 0:V_DIM, :] = v_t.reshape(N_HEADS, V_DIM, t).astype(BF16)
    vt_ref[0, :, V_DIM:V_ROWS, :] = jnp.ones((N_HEADS, V_ROWS - V_DIM, t), BF16)

    lat = jnp.dot(xb, wtok_ref[...], preferred_element_type=F32)
    ckv_tok = lat[:, 0:KV_LORA]
    ckv_tok = ckv_tok * lax.rsqrt(jnp.mean(ckv_tok * ckv_tok, axis=-1, keepdims=True) + RMS_EPS) * kvnr_ref[...]
    k_rope = lat[:, KV_LORA:KV_LORA + LANES] * cosp_ref[...] + lat[:, KV_LORA + LANES:] * sinp_ref[...]
    kn = jnp.dot(ckv_tok.astype(BF16), wkn_ref[...], preferred_element_type=F32)
    for h in range(N_HEADS):
        k_ref[0, h] = (kn[:, h * HEAD_PAD:(h + 1) * HEAD_PAD] + k_rope).astype(BF16)


def _mla_proj(x2d, tabs, w, batch, seq):
    n, d = x2d.shape
    t = TOK_TILE
    tiles_per_seq = seq // t
    q_scale = (QK_NOPE + QK_ROPE) ** -0.5 * math.log2(math.e)
    full = lambda a: pl.BlockSpec(a.shape, lambda i: (0,) * a.ndim)
    bs_idx = lambda i: (i // tiles_per_seq, 0, 0, i % tiles_per_seq)
    return pl.pallas_call(
        functools.partial(_mla_proj_kernel, q_scale=q_scale),
        grid=(n // t,),
        in_specs=[
            pl.BlockSpec((t, d), lambda i: (i, 0)),
            pl.BlockSpec((1, HALF_ROPE, t), lambda i: (i // tiles_per_seq, 0, i % tiles_per_seq)),
            pl.BlockSpec((1, HALF_ROPE, t), lambda i: (i // tiles_per_seq, 0, i % tiles_per_seq)),
            pl.BlockSpec((t, LANES), lambda i: (i, 0)),
            pl.BlockSpec((t, LANES), lambda i: (i, 0)),
            full(w['w_fm']), full(w['w_tok']), full(w['w_uq_t']), full(w['w_v_t']), full(w['w_kn']),
            full(w['qn_col']), full(w['kvn_col']), full(w['kvn_row']),
        ],
        out_specs=[
            pl.BlockSpec((1, N_HEADS, HEAD_PAD, t), bs_idx),
            pl.BlockSpec((1, N_HEADS, t, HEAD_PAD), lambda i: (i // tiles_per_seq, 0, i % tiles_per_seq, 0)),
            pl.BlockSpec((1, N_HEADS, V_ROWS, t), bs_idx),
        ],
        out_shape=[
            jax.ShapeDtypeStruct((batch, N_HEADS, HEAD_PAD, seq), BF16),
            jax.ShapeDtypeStruct((batch, N_HEADS, seq, HEAD_PAD), BF16),
            jax.ShapeDtypeStruct((batch, N_HEADS, V_ROWS, seq), BF16),
        ],
        compiler_params=pltpu.CompilerParams(dimension_semantics=("arbitrary",),
                                             vmem_limit_bytes=56 * 1024 * 1024),
        name="mla_proj",
    )(x2d, tabs['cos_t'], tabs['sin_t'], tabs['cos_pad'], tabs['sin_pad'],
      w['w_fm'], w['w_tok'], w['w_uq_t'], w['w_v_t'], w['w_kn'], w['qn_col'], w['kvn_col'], w['kvn_row'])


def _attn_kernel(qt_ref, k_ref, vt_ref, o_ref, s_scr, p_scr):
    n_h = qt_ref.shape[1]
    seq = k_ref.shape[2]
    tq, tk = ATT_TQ, ATT_TK
    n_q = seq // tq
    heads = range(n_h)

    def load_q(q0):
        return [qt_ref[0, h, :, pl.ds(q0, tq)] for h in heads]

    def scores(h, q_t, k0, slot):
        kb = k_ref[0, h, pl.ds(k0, tk), :]
        s = jnp.dot(kb, q_t, preferred_element_type=F32)
        s_scr[h, slot] = s
        return jnp.max(s, axis=0, keepdims=True)

    def softmax(h, slot, m, bmax):
        s = s_scr[h, slot]
        m_new = jnp.maximum(m, bmax)
        p_scr[h, slot] = jnp.exp2(s - m_new).astype(BF16)
        return m_new, jnp.exp2(m - m_new)

    def values(h, k0, slot, alpha, acc):
        vb = vt_ref[0, h, :, pl.ds(k0, tk)]
        return alpha * acc + jnp.dot(vb, p_scr[h, slot], preferred_element_type=F32)

    def trip(t, carry, q_ts):
        m, acc, a_pend, bmax0 = [list(c) for c in carry]
        ka = pl.multiple_of(t * tq, tq)
        k_prev = pl.multiple_of(jnp.maximum(ka - tk, 0), tk)
        a0, bmax1 = [None] * n_h, [None] * n_h
        for h in heads:
            bmax1[h] = scores(h, q_ts[h], ka + tk, 1)
        for h in heads:
            m[h], a0[h] = softmax(h, 0, m[h], bmax0[h])
        for h in heads:
            acc[h] = values(h, k_prev, 1, a_pend[h], acc[h])
        for h in heads:
            bmax0[h] = scores(h, q_ts[h], ka + tq, 0)
        for h in heads:
            m[h], a_pend[h] = softmax(h, 1, m[h], bmax1[h])
        for h in heads:
            acc[h] = values(h, ka, 0, a0[h], acc[h])
        return tuple(m), tuple(acc), tuple(a_pend), tuple(bmax0)

    def diagonal_trip(q0, carry, q_ts, q_next):
        m, acc, a_pend, _ = carry
        k_prev = pl.multiple_of(jnp.maximum(q0 - tk, 0), tk)
        tri = lax.broadcasted_iota(I32, (tk, tk), 0) <= lax.broadcasted_iota(I32, (tk, tk), 1)
        bmax_next = []
        for h in heads:
            kb = k_ref[0, h, pl.ds(q0 + tk, tk), :]
            s_b = jnp.dot(kb, q_ts[h][:, tk:], preferred_element_type=F32)
            s_a = s_scr[h, 0]
            s_al = jnp.where(tri, s_a[:, :tk], NEG_BIG)
            s_ar = s_a[:, tk:]
            bmax_a = jnp.concatenate([jnp.max(s_al, axis=0, keepdims=True),
                                      jnp.max(s_ar, axis=0, keepdims=True)], axis=1)
            m_a = jnp.maximum(m[h], bmax_a)
            a0 = jnp.exp2(m[h] - m_a)
            p_scr[h, 0] = jnp.concatenate([jnp.exp2(s_al - m_a[:, :tk]), jnp.exp2(s_ar - m_a[:, tk:])],
                                          axis=1).astype(BF16)
            acc_h = values(h, k_prev, 1, a_pend[h], acc[h])
            bmax_next.append(scores(h, q_next[h], 0, 0))
            s_b = jnp.where(tri, s_b, NEG_BIG)
            m_r = jnp.maximum(m_a[:, tk:], jnp.max(s_b, axis=0, keepdims=True))
            p_b = jnp.exp2(s_b - m_r).astype(BF16)
            acc_h = values(h, q0, 0, a0, acc_h)
            vb = vt_ref[0, h, :, pl.ds(q0 + tk, tk)]
            acc_r = jnp.exp2(m_a[:, tk:] - m_r) * acc_h[:, tk:] + jnp.dot(vb, p_b, preferred_element_type=F32)
            out = jnp.concatenate([acc_h[:, :tk], acc_r], axis=1)
            o_ref[0, h * V_DIM:(h + 1) * V_DIM, pl.ds(q0, tq)] = (
                out[0:V_DIM] / out[V_DIM:V_DIM + 1]).astype(BF16)
        return tuple(bmax_next)

    def q_tile(qi, bmax0):
        q0 = pl.multiple_of(qi * tq, tq)
        q_ts = load_q(q0)
        q_next = load_q(pl.multiple_of(jnp.minimum(q0 + tq, seq - tq), tq))
        for h in heads:
            p_scr[h, 1] = jnp.zeros((tk, tq), BF16)
        init = (tuple(jnp.full((1, tq), NEG_BIG, F32) for _ in heads),
                tuple(jnp.zeros((V_ROWS, tq), F32) for _ in heads),
                tuple(jnp.ones((1, tq), F32) for _ in heads),
                bmax0)
        res = lax.fori_loop(0, qi, lambda t, c: trip(t, c, q_ts), init)
        return diagonal_trip(q0, res, q_ts, q_next)

    q_first = load_q(0)
    lax.fori_loop(0, n_q, q_tile, tuple(scores(h, q_first[h], 0, 0) for h in heads))


def _attention(q_t, k, v_t):
    batch, heads, _, seq = q_t.shape
    g = ATT_HEADS_PER_STEP
    assert ATT_TQ == 2 * ATT_TK and seq % ATT_TQ == 0 and heads % g == 0
    return pl.pallas_call(
        _attn_kernel,
        grid=(batch, heads // g),
        in_specs=[
            pl.BlockSpec((1, g, HEAD_PAD, seq), lambda b, h: (b, h, 0, 0)),
            pl.BlockSpec((1, g, seq, HEAD_PAD), lambda b, h: (b, h, 0, 0)),
            pl.BlockSpec((1, g, V_ROWS, seq), lambda b, h: (b, h, 0, 0)),
        ],
        out_specs=pl.BlockSpec((1, g * V_DIM, seq), lambda b, h: (b, h, 0)),
        out_shape=jax.ShapeDtypeStruct((batch, heads * V_DIM, seq), BF16),
        scratch_shapes=[pltpu.VMEM((g, 2, ATT_TK, ATT_TQ), F32), pltpu.VMEM((g, 2, ATT_TK, ATT_TQ), BF16)],
        compiler_params=pltpu.CompilerParams(dimension_semantics=("arbitrary", "arbitrary"),
                                             vmem_limit_bytes=56 * 1024 * 1024),
        name="mla_attention",
    )(q_t, k, v_t)


def _outproj_ln_kernel(ot_ref, x_ref, wo_ref, g_ref, b_ref, out_ref, *, alpha):
    f = lax.dot_general(ot_ref[0], wo_ref[...], _TN, preferred_element_type=F32)
    out_ref[...] = _layer_norm(alpha * x_ref[...] + f, g_ref[...], b_ref[...])


def _outproj_ln(o_t, x2d, w_o, g, b, alpha):
    n, d = x2d.shape
    batch, hd, seq = o_t.shape
    t = TOK_TILE
    tiles_per_seq = seq // t
    return pl.pallas_call(
        functools.partial(_outproj_ln_kernel, alpha=alpha),
        grid=(n // t,),
        in_specs=[
            pl.BlockSpec((1, hd, t), lambda i: (i // tiles_per_seq, 0, i % tiles_per_seq)),
            pl.BlockSpec((t, d), lambda i: (i, 0)),
            pl.BlockSpec((hd, d), lambda i: (0, 0)),
            pl.BlockSpec((1, d), lambda i: (0, 0)),
            pl.BlockSpec((1, d), lambda i: (0, 0)),
        ],
        out_specs=pl.BlockSpec((t, d), lambda i: (i, 0)),
        out_shape=jax.ShapeDtypeStruct((n, d), F32),
        compiler_params=pltpu.CompilerParams(dimension_semantics=("arbitrary",)),
        name="mla_outproj_ln",
    )(o_t, x2d, w_o, g, b)


def _pool_ln_kernel(x_ref, halo_ref, w_ref, pb_ref, ps_ref, g_ref, b_ref, out_ref, *, alpha, tiles_per_seq):
    t, d = x_ref.shape
    gd = d // len(POOL_WINDOWS)
    i = pl.program_id(0)
    tile_in_seq = i % tiles_per_seq
    x = x_ref[...]
    halo = jnp.where(tile_in_seq == 0, 0.0, halo_ref[...])
    ext = jnp.concatenate([halo, x], axis=0)
    pos = tile_in_seq * t + lax.broadcasted_iota(I32, (t, 1), 0)
    ys = []
    for gi, win in enumerate(POOL_WINDOWS):
        e = ext[:, gi * gd:(gi + 1) * gd]
        sh = 1
        while sh < win:
            e = e + pltpu.roll(e, sh, axis=0)
            sh *= 2
        inv_cnt = 1.0 / jnp.minimum(pos + 1, win).astype(F32)
        pooled = e[POOL_HALO:] * inv_cnt - x[:, gi * gd:(gi + 1) * gd]
        ys.append(jnp.dot(pooled.astype(BF16), w_ref[gi], preferred_element_type=F32))
    y = (jnp.concatenate(ys, axis=1) + pb_ref[...]) * ps_ref[...]
    out_ref[...] = _layer_norm(alpha * x + y, g_ref[...], b_ref[...])


def _pool_ln(x2d, seq, w, pb, ps, g, b, alpha):
    n, d = x2d.shape
    t = TOK_TILE
    tiles_per_seq = seq // t
    halo_blocks = t // POOL_HALO
    vec = pl.BlockSpec((1, d), lambda i: (0, 0))
    return pl.pallas_call(
        functools.partial(_pool_ln_kernel, alpha=alpha, tiles_per_seq=tiles_per_seq),
        grid=(n // t,),
        in_specs=[
            pl.BlockSpec((t, d), lambda i: (i, 0)),
            pl.BlockSpec((POOL_HALO, d), lambda i: (jnp.maximum(i * halo_blocks - 1, 0), 0)),
            pl.BlockSpec(w.shape, lambda i: (0, 0, 0)),
            vec, vec, vec, vec,
        ],
        out_specs=pl.BlockSpec((t, d), lambda i: (i, 0)),
        out_shape=jax.ShapeDtypeStruct((n, d), F32),
        compiler_params=pltpu.CompilerParams(dimension_semantics=("arbitrary",)),
        name="pool_ln",
    )(x2d, x2d, w, pb, ps, g, b)


def _router_kernel(x_ref, whi_ref, wlo_ref, bias_ref, tri_ref, eid_ref, gate_ref, rank_ref, cnt_ref, run_ref):
    t = x_ref.shape[0]

    @pl.when(pl.program_id(0) == 0)
    def _():
        run_ref[...] = jnp.zeros_like(run_ref)

    x = x_ref[...]
    xh = x.astype(BF16)
    xl = (x - xh.astype(F32)).astype(BF16)
    lg = (lax.dot_general(whi_ref[...], xh, _NT, preferred_element_type=F32)
          + lax.dot_general(whi_ref[...], xl, _NT, preferred_element_type=F32)
          + lax.dot_general(wlo_ref[...], xh, _NT, preferred_element_type=F32)
          + bias_ref[...])
    ridx = lax.broadcasted_iota(I32, (N_GROUPS, t), 0)

    g = lg[0:N_GROUPS]
    gmax = jnp.max(g, axis=0, keepdims=True)
    g_sel = jnp.min(jnp.where(g == gmax, ridx, N_GROUPS), axis=0, keepdims=True)
    g_gate = 1.0 / jnp.sum(jnp.exp(g - gmax), axis=0, keepdims=True)

    e_all = lg[N_GROUPS:N_GROUPS + N_EXPERTS]
    e = jnp.zeros((EXP_PER_GROUP, t), F32)
    for gi in range(N_GROUPS):
        e = e + jnp.where(g_sel == gi, e_all[gi * EXP_PER_GROUP:(gi + 1) * EXP_PER_GROUP], 0.0)
    pe = jnp.exp(e - jnp.max(e, axis=0, keepdims=True))
    p1 = jnp.max(pe, axis=0, keepdims=True)
    i1 = jnp.min(jnp.where(pe == p1, ridx, EXP_PER_GROUP), axis=0, keepdims=True)
    pe2 = jnp.where(ridx == i1, -1.0, pe)
    p2 = jnp.max(pe2, axis=0, keepdims=True)
    i2 = jnp.min(jnp.where(pe2 == p2, ridx, EXP_PER_GROUP), axis=0, keepdims=True)
    denom = p1 + p2
    e1 = g_sel * EXP_PER_GROUP + i1
    e2 = g_sel * EXP_PER_GROUP + i2
    eid_ref[0:1, :] = e1
    eid_ref[1:2, :] = e2
    gate_ref[0:1, :] = g_gate * p1 / denom
    gate_ref[1:2, :] = g_gate * p2 / denom

    eidx = lax.broadcasted_iota(I32, (N_EXPERTS, t), 0)
    oh1 = eidx == e1
    oh2 = eidx == e2
    oh = jnp.where(oh1 | oh2, 1.0, 0.0)
    earlier = jnp.dot(oh.astype(BF16), tri_ref[...], preferred_element_type=F32)
    tot = earlier + run_ref[:, 0:1]
    rank_ref[0:1, :] = jnp.sum(jnp.where(oh1, tot, 0.0), axis=0, keepdims=True).astype(I32)
    rank_ref[1:2, :] = jnp.sum(jnp.where(oh2, tot, 0.0), axis=0, keepdims=True).astype(I32)
    run_ref[...] = run_ref[...] + jnp.sum(oh, axis=1, keepdims=True)
    cnt_ref[...] = run_ref[...].astype(I32)


def _router(x2d, w_hi, w_lo, bias, tri):
    n, d = x2d.shape
    t = ROUTE_TILE
    tok = lambda dt: jax.ShapeDtypeStruct((TOP_K, n), dt)
    tok_spec = pl.BlockSpec((TOP_K, t), lambda i: (0, i))
    return pl.pallas_call(
        _router_kernel,
        grid=(n // t,),
        in_specs=[
            pl.BlockSpec((t, d), lambda i: (i, 0)),
            pl.BlockSpec(w_hi.shape, lambda i: (0, 0)),
            pl.BlockSpec(w_lo.shape, lambda i: (0, 0)),
            pl.BlockSpec(bias.shape, lambda i: (0, 0)),
            pl.BlockSpec(tri.shape, lambda i: (0, 0)),
        ],
        out_specs=[tok_spec, tok_spec, tok_spec, pl.BlockSpec((N_EXPERTS, LANES), lambda i: (0, 0))],
        out_shape=[tok(I32), tok(F32), tok(I32), jax.ShapeDtypeStruct((N_EXPERTS, LANES), I32)],
        scratch_shapes=[pltpu.VMEM((N_EXPERTS, LANES), F32)],
        compiler_params=pltpu.CompilerParams(dimension_semantics=("arbitrary",)),
        name="moe_router",
    )(x2d, w_hi, w_lo, bias, tri)


ROWS_PER_TRIP = 8


def _pack_halves(y):
    half = y.shape[1] // 2
    bits = lax.bitcast_convert_type(y.astype(BF16).astype(F32), jnp.uint32)
    return bits[:, :half] | (bits[:, half:] >> 16)


def _unpack_halves(p):
    hi = lax.bitcast_convert_type(p & jnp.uint32(0xFFFF0000), F32)
    lo = lax.bitcast_convert_type(p << 16, F32)
    return hi, lo


def _for_row_groups(n_rows, body):
    def trip(g, c):
        base = g * ROWS_PER_TRIP
        for u in range(ROWS_PER_TRIP):
            body(g, u, base + u)
        return c
    lax.fori_loop(0, n_rows // ROWS_PER_TRIP, trip, 0)


def _staged_row(buf, g, u):
    return buf.at[g, pl.ds(u, 1)]


def _hbm_row(arr, row):
    return arr.at[pl.ds(row, 1)]


def _slots_kernel(pstart_ref, eid_ref, rank_ref, dest_ref):
    eid = eid_ref[...]

    def add_expert(e, acc):
        return acc + jnp.where(eid == e, pstart_ref[e], 0)

    dest_ref[...] = lax.fori_loop(0, N_EXPERTS, add_expert, rank_ref[...])


def _slots(pstart, eid, rank):
    full = pl.BlockSpec(eid.shape, lambda i, ps: (0, 0))
    return pl.pallas_call(
        _slots_kernel,
        grid_spec=pltpu.PrefetchScalarGridSpec(num_scalar_prefetch=1, grid=(1,),
                                               in_specs=[full, full], out_specs=full),
        out_shape=jax.ShapeDtypeStruct(eid.shape, I32),
        name="moe_slots",
    )(pstart, eid, rank)


def _dispatch_kernel(d0_ref, d1_ref, x_ref, rows_in_hbm, rows_hbm, xp_scr, sem):
    del rows_in_hbm
    t = x_ref.shape[0]
    xp_scr[...] = _pack_halves(x_ref[...]).reshape(xp_scr.shape)

    def issue(g, u, r):
        for k, d_ref in enumerate((d0_ref, d1_ref)):
            pltpu.make_async_copy(_staged_row(xp_scr, g, u), _hbm_row(rows_hbm, d_ref[r]), sem).start(priority=k)

    def drain(g, u, r):
        for _ in range(TOP_K):
            pltpu.make_async_copy(_staged_row(xp_scr, 0, 0), _hbm_row(rows_hbm, 0), sem).wait()

    _for_row_groups(t, issue)
    _for_row_groups(t, drain)


def _dispatch(dest, x2d, rows_init):
    n, d = x2d.shape
    t = DISPATCH_TILE
    tok_spec = pl.BlockSpec((t,), lambda i: (i,), memory_space=pltpu.SMEM)
    return pl.pallas_call(
        _dispatch_kernel,
        grid=(n // t,),
        in_specs=[
            tok_spec, tok_spec,
            pl.BlockSpec((t, d), lambda i: (i, 0)),
            pl.BlockSpec(memory_space=pl.ANY),
        ],
        out_specs=pl.BlockSpec(memory_space=pl.ANY),
        out_shape=jax.ShapeDtypeStruct(rows_init.shape, rows_init.dtype),
        scratch_shapes=[pltpu.VMEM((t // ROWS_PER_TRIP, ROWS_PER_TRIP, d // 2), jnp.uint32),
                        pltpu.SemaphoreType.DMA(())],
        input_output_aliases={3: 0},
        compiler_params=pltpu.CompilerParams(dimension_semantics=("arbitrary",)),
        name="moe_dispatch",
    )(dest[0], dest[1], x2d, rows_init)


def _ffn_kernel(be_ref, nu_ref, rows_ref, w1_ref, w3_ref, w2_ref, y_ref, w1b, w3b, w2b):
    b = pl.program_id(0)
    half = rows_ref.shape[1]

    @pl.when(b < nu_ref[0])
    def _():
        prev = be_ref[jnp.maximum(b - 1, 0)]

        @pl.when((b == 0) | (be_ref[b] != prev))
        def _():
            w1b[...] = w1_ref[0, 0].astype(BF16)
            w3b[...] = w3_ref[0, 0].astype(BF16)
            w2b[...] = w2_ref[0, 0].astype(BF16)

        for c in range(rows_ref.shape[0] // FFN_CHUNK):
            rows = pl.ds(c * FFN_CHUNK, FFN_CHUNK)
            x_hi, x_lo = _unpack_halves(rows_ref[rows, :])
            x_hi, x_lo = x_hi.astype(BF16), x_lo.astype(BF16)

            def up(w):
                return (jnp.dot(x_hi, w[0:half, :], preferred_element_type=F32)
                        + jnp.dot(x_lo, w[half:2 * half, :], preferred_element_type=F32))

            h1 = up(w1b)
            hb = h1 * jax.nn.sigmoid(h1) * up(w3b)
            y_ref[rows, :] = _pack_halves(jnp.dot(hb.astype(BF16), w2b[...], preferred_element_type=F32))


def _expert_ffn(blk_e, n_used, rows, w1, w3, w2, layer):
    r, half = rows.shape
    d, de = w1.shape[2], w1.shape[3]
    bm = EXPERT_BLOCK
    nb = r // bm
    clamp = lambda b, nu: jnp.minimum(b, nu[0] - 1)
    row_spec = pl.BlockSpec((bm, half), lambda b, be, nu: (clamp(b, nu), 0))
    w_idx = lambda b, be, nu: (layer, be[clamp(b, nu)], 0, 0)
    return pl.pallas_call(
        _ffn_kernel,
        grid_spec=pltpu.PrefetchScalarGridSpec(
            num_scalar_prefetch=2,
            grid=(nb,),
            in_specs=[
                row_spec,
                pl.BlockSpec((1, 1, d, de), w_idx),
                pl.BlockSpec((1, 1, d, de), w_idx),
                pl.BlockSpec((1, 1, de, d), w_idx),
            ],
            out_specs=row_spec,
            scratch_shapes=[pltpu.VMEM((d, de), BF16), pltpu.VMEM((d, de), BF16), pltpu.VMEM((de, d), BF16)],
        ),
        out_shape=jax.ShapeDtypeStruct((r, half), jnp.uint32),
        compiler_params=pltpu.CompilerParams(dimension_semantics=("arbitrary",)),
        name="moe_expert_ffn",
    )(blk_e, n_used, rows, w1, w3, w2)


def _combine_ln_kernel(d0_ref, d1_ref, n0_ref, n1_ref, gate_ref, x_ref, y_hbm, g_ref, b_ref, out_ref,
                       ybuf, sems, *, alpha):
    t = COMBINE_TILE
    i = pl.program_id(0)

    def request_row(slot, d_refs, row, g, u):
        for k, d_ref in enumerate(d_refs):
            pltpu.make_async_copy(_hbm_row(y_hbm, d_ref[row]), _staged_row(ybuf.at[slot, k], g, u),
                                  sems.at[slot]).start(priority=k)

    def wait_tile(slot):
        def wait(g, u, r):
            for k in range(TOP_K):
                pltpu.make_async_copy(_hbm_row(y_hbm, 0), _staged_row(ybuf.at[slot, k], 0, 0), sems.at[slot]).wait()
        _for_row_groups(t, wait)

    def request_tile(slot, d_refs, first_row):
        _for_row_groups(t, lambda g, u, r: request_row(slot, d_refs, first_row + r, g, u))

    def reduce_tile(slot, first_row):
        wait_tile(slot)
        rows = pl.ds(first_row, t)
        gate = gate_ref[rows, :]
        hi0, lo0 = _unpack_halves(ybuf[slot, 0].reshape(t, -1))
        hi1, lo1 = _unpack_halves(ybuf[slot, 1].reshape(t, -1))
        g0, g1 = gate[:, 0:1], gate[:, 1:2]
        f = jnp.concatenate([g0 * hi0 + g1 * hi1, g0 * lo0 + g1 * lo1], axis=1)
        out_ref[rows, :] = _layer_norm(alpha * x_ref[rows, :] + f, g_ref[...], b_ref[...])

    @pl.when(i == 0)
    def _():
        request_tile(0, (d0_ref, d1_ref), 0)

    request_tile(1, (d0_ref, d1_ref), t)
    reduce_tile(0, 0)

    @pl.when(i + 1 < pl.num_programs(0))
    def _():
        request_tile(0, (n0_ref, n1_ref), 0)

    reduce_tile(1, t)


def _combine_ln(dest, gates_tok, x2d, y_rows, g, b, alpha):
    n, d = x2d.shape
    t = COMBINE_TILE
    n_tiles = n // t
    vec = pl.BlockSpec((1, d), lambda i: (0, 0))
    pair_spec = pl.BlockSpec((2 * t,), lambda i: (i,), memory_space=pltpu.SMEM)
    next_spec = pl.BlockSpec((t,), lambda i: (jnp.minimum(2 * i + 2, n_tiles - 1),), memory_space=pltpu.SMEM)
    return pl.pallas_call(
        functools.partial(_combine_ln_kernel, alpha=alpha),
        grid=(n_tiles // 2,),
        in_specs=[
            pair_spec, pair_spec, next_spec, next_spec,
            pl.BlockSpec((2 * t, TOP_K), lambda i: (i, 0)),
            pl.BlockSpec((2 * t, d), lambda i: (i, 0)),
            pl.BlockSpec(memory_space=pl.ANY),
            vec, vec,
        ],
        out_specs=pl.BlockSpec((2 * t, d), lambda i: (i, 0)),
        out_shape=jax.ShapeDtypeStruct((n, d), F32),
        scratch_shapes=[pltpu.VMEM((2, TOP_K, t // ROWS_PER_TRIP, ROWS_PER_TRIP, d // 2), jnp.uint32),
                        pltpu.SemaphoreType.DMA((2,))],
        compiler_params=pltpu.CompilerParams(dimension_semantics=("arbitrary",)),
        name="moe_combine_ln",
    )(dest[0], dest[1], dest[0], dest[1], gates_tok, x2d, y_rows, g, b)


def _rope_tables(positions):
    inv_freq = ROPE_THETA ** (-jnp.arange(0, QK_ROPE, 2, dtype=F32) / QK_ROPE)
    ang = positions.astype(F32)[..., None] * inv_freq
    cos, sin = jnp.cos(ang), jnp.sin(ang)
    b, s, _ = cos.shape
    zeros = jnp.zeros((b * s, QK_NOPE), F32)
    tail = jnp.zeros((b * s, LANES - QK_NOPE - QK_ROPE), F32)
    c2, s2 = cos.reshape(b * s, -1), sin.reshape(b * s, -1)
    return {
        'cos_t': jnp.swapaxes(cos, 1, 2), 'sin_t': jnp.swapaxes(sin, 1, 2),
        'cos_pad': jnp.concatenate([zeros, c2, c2, tail], axis=1),
        'sin_pad': jnp.concatenate([zeros, s2, s2, tail], axis=1),
    }


def _mla_weights(w_in, q_norm, w_uq, kv_norm, w_ukv, w_o):
    d = w_in.shape[0]
    lat_q_kv = Q_LORA + KV_LORA
    r1 = w_in[:, lat_q_kv:lat_q_kv + HALF_ROPE]
    r2 = w_in[:, lat_q_kv + HALF_ROPE:lat_q_kv + QK_ROPE]
    z_lo = jnp.zeros((d, QK_NOPE), F32)
    z_hi = jnp.zeros((d, LANES - QK_NOPE - QK_ROPE), F32)
    w_tok = jnp.concatenate([w_in[:, Q_LORA:lat_q_kv],
                             z_lo, r1, r2, z_hi,
                             z_lo, -r2, r1, z_hi],
                            axis=1)
    uq = w_uq.reshape(Q_LORA, N_HEADS, QK_NOPE + QK_ROPE)
    w_uq_t = jnp.concatenate([uq[:, :, :QK_NOPE].reshape(Q_LORA, -1),
                              uq[:, :, QK_NOPE:QK_NOPE + HALF_ROPE].reshape(Q_LORA, -1),
                              uq[:, :, QK_NOPE + HALF_ROPE:].reshape(Q_LORA, -1)], axis=1).T
    ukv = w_ukv.reshape(KV_LORA, N_HEADS, QK_NOPE + V_DIM)
    w_kn = jnp.concatenate([ukv[:, :, :QK_NOPE], jnp.zeros((KV_LORA, N_HEADS, HEAD_PAD - QK_NOPE), F32)],
                           axis=2).reshape(KV_LORA, N_HEADS * HEAD_PAD)
    w_v_t = ukv[:, :, QK_NOPE:].reshape(KV_LORA, N_HEADS * V_DIM).T
    return {
        'w_fm': w_in[:, :lat_q_kv].T.astype(BF16), 'w_tok': w_tok.astype(BF16),
        'w_uq_t': w_uq_t.astype(BF16), 'w_v_t': w_v_t.astype(BF16), 'w_kn': w_kn.astype(BF16),
        'qn_col': q_norm.reshape(-1, 1), 'kvn_col': kv_norm.reshape(-1, 1), 'kvn_row': kv_norm.reshape(1, -1),
        'w_o': w_o.astype(BF16),
    }


def _router_weights(w_grp, b_grp, w_exp, b_exp):
    d = w_grp.shape[0]
    pad = LANES - N_GROUPS - N_EXPERTS
    w = jnp.concatenate([w_grp, w_exp, jnp.zeros((d, pad), F32)], axis=1).T
    w_hi = w.astype(BF16)
    w_lo = (w - w_hi.astype(F32)).astype(BF16)
    bias = jnp.concatenate([b_grp, b_exp, jnp.zeros((pad,), F32)]).reshape(-1, 1)
    return w_hi, w_lo, bias


def _moe_ln(x2d, w_hi, w_lo, bias, tri, w1, w3, w2, layer, g, b, alpha):
    n, d = x2d.shape
    bm = EXPERT_BLOCK
    n_assign = n * TOP_K
    nb = -(-(n_assign + N_EXPERTS * (bm - 1)) // bm)
    eid, gates, rank, cnt = _router(x2d, w_hi, w_lo, bias, tri)
    counts = cnt[:, 0]
    padded = (counts + bm - 1) // bm * bm
    pend = jnp.cumsum(padded)
    pstart = (pend - padded).astype(I32)
    n_used = (pend[-1:] // bm).astype(I32)
    blk_start = jnp.arange(nb, dtype=I32) * bm
    blk_e = jnp.minimum(jnp.sum((pend[None, :] <= blk_start[:, None]).astype(I32), axis=1), N_EXPERTS - 1)
    dest = _slots(pstart, eid, rank)
    rows = _dispatch(dest, x2d, jnp.zeros((nb * bm, d // 2), jnp.uint32))
    y_rows = _expert_ffn(blk_e, n_used, rows, w1, w3, w2, layer)
    return _combine_ln(dest, gates.T, x2d, y_rows, g, b, alpha)


def kernel(x, positions, ln_g, ln_b, mla_w_in, mla_q_norm, mla_w_uq, mla_kv_norm, mla_w_ukv, mla_w_o,
           pool_w, pool_b, pool_scale, moe_w_grp, moe_b_grp, moe_w_exp, moe_b_exp, moe_w1, moe_w3, moe_w2):
    batch, seq, d = x.shape
    depth = ln_g.shape[0]
    alpha = (2 * depth) ** 0.25
    tabs = _rope_tables(positions)
    t = ROUTE_TILE
    tri = (lax.broadcasted_iota(I32, (t, t), 0) < lax.broadcasted_iota(I32, (t, t), 1)).astype(BF16)
    x2d = x.reshape(batch * seq, d)
    vec = lambda v: v.reshape(1, d)
    for i in range(depth):
        j = i // 2
        if i % 2 == 0:
            w = _mla_weights(mla_w_in[j], mla_q_norm[j], mla_w_uq[j], mla_kv_norm[j], mla_w_ukv[j], mla_w_o[j])
            q_t, k, v_t = _mla_proj(x2d, tabs, w, batch, seq)
            o_t = _attention(q_t, k, v_t)
            x2d = _outproj_ln(o_t, x2d, w['w_o'], vec(ln_g[i, 0]), vec(ln_b[i, 0]), alpha)
        else:
            x2d = _pool_ln(x2d, seq, pool_w[j].astype(BF16), vec(pool_b[j]), vec(pool_scale[j]),
                           vec(ln_g[i, 0]), vec(ln_b[i, 0]), alpha)
        w_hi, w_lo, bias = _router_weights(moe_w_grp[i], moe_b_grp[i], moe_w_exp[i], moe_b_exp[i])
        x2d = _moe_ln(x2d, w_hi, w_lo, bias, tri, moe_w1, moe_w3, moe_w2, i,
                      vec(ln_g[i, 1]), vec(ln_b[i, 1]), alpha)
    return x2d.reshape(batch, seq, d)
```

```python
import functools
import math

import jax
import jax.numpy as jnp
from jax import lax
from jax.experimental import pallas as pl
from jax.experimental.pallas import tpu as pltpu

F32 = jnp.float32
BF16 = jnp.bfloat16
I32 = jnp.int32

N_HEADS = 16
QK_NOPE = 64
QK_ROPE = 32
HALF_ROPE = QK_ROPE // 2
V_DIM = 64
Q_LORA = 256
KV_LORA = 128
ROPE_THETA = 10000.0
POOL_WINDOWS = (2, 4, 8, 16)
N_GROUPS = 8
EXP_PER_GROUP = 8
N_EXPERTS = N_GROUPS * EXP_PER_GROUP
TOP_K = 2
LN_EPS = 1e-5
RMS_EPS = 1e-6

LANES = 128
HEAD_PAD = 128
V_ROWS = 80
POOL_HALO = 16

TOK_TILE = 512
LN_TILE = 1024
ATT_TQ = 1024
ATT_TK = 512
ATT_HEADS_PER_STEP = 1
ROUTE_TILE = 512
DISPATCH_TILE = 1024
COMBINE_TILE = 512
EXPERT_BLOCK = 512
FFN_CHUNK = 256

NEG_BIG = -1e30
_NT = (((1,), (1,)), ((), ()))
_TN = (((0,), (0,)), ((), ()))


def _layer_norm(y, g, b):
    mu = jnp.mean(y, axis=-1, keepdims=True)
    yc = y - mu
    var = jnp.mean(yc * yc, axis=-1, keepdims=True)
    return yc * lax.rsqrt(var + LN_EPS) * g + b


def _mla_proj_kernel(x_ref, cost_ref, sint_ref, cosp_ref, sinp_ref, wfm_ref, wtok_ref, wuq_ref,
                     wv_ref, wkn_ref, qn_ref, kvnc_ref, kvnr_ref, qt_ref, k_ref, vt_ref, *, q_scale):
    t = x_ref.shape[0]
    xb = x_ref[...].astype(BF16)
    lat_t = lax.dot_general(wfm_ref[...], xb, _NT, preferred_element_type=F32)
    cq = lat_t[0:Q_LORA]
    cq = cq * lax.rsqrt(jnp.mean(cq * cq, axis=0, keepdims=True) + RMS_EPS) * qn_ref[...]
    ckv = lat_t[Q_LORA:Q_LORA + KV_LORA]
    ckv = ckv * lax.rsqrt(jnp.mean(ckv * ckv, axis=0, keepdims=True) + RMS_EPS) * kvnc_ref[...]

    q_t = jnp.dot(wuq_ref[...], cq.astype(BF16), preferred_element_type=F32) * q_scale
    n_nope = N_HEADS * QK_NOPE
    n_half = N_HEADS * HALF_ROPE
    qn = q_t[0:n_nope].reshape(N_HEADS, QK_NOPE, t)
    x1 = q_t[n_nope:n_nope + n_half].reshape(N_HEADS, HALF_ROPE, t)
    x2 = q_t[n_nope + n_half:n_nope + 2 * n_half].reshape(N_HEADS, HALF_ROPE, t)
    c = cost_ref[...]
    s = sint_ref[...]
    qt_ref[0, :, 0:QK_NOPE, :] = qn.astype(BF16)
    qt_ref[0, :, QK_NOPE:QK_NOPE + HALF_ROPE, :] = (x1 * c - x2 * s).astype(BF16)
    qt_ref[0, :, QK_NOPE + HALF_ROPE:QK_NOPE + QK_ROPE, :] = (x2 * c + x1 * s).astype(BF16)
    qt_ref[0, :, QK_NOPE + QK_ROPE:HEAD_PAD, :] = jnp.zeros(
        (N_HEADS, HEAD_PAD - QK_NOPE - QK_ROPE, t), BF16)

    v_t = jnp.dot(wv_ref[...], ckv.astype(BF16), preferred_element_type=F32)
    vt_ref[0, :, 0:V_DIM, :] = v_t.reshape(N_HEADS, V_DIM, t).astype(BF16)
    vt_ref[0, :, V_DIM:V_ROWS, :] = jnp.ones((N_HEADS, V_ROWS - V_DIM, t), BF16)

    lat = jnp.dot(xb, wtok_ref[...], preferred_element_type=F32)
    ckv_tok = lat[:, 0:KV_LORA]
    ckv_tok = ckv_tok * lax.rsqrt(jnp.mean(ckv_tok * ckv_tok, axis=-1, keepdims=True) + RMS_EPS) * kvnr_ref[...]
    k_rope = lat[:, KV_LORA:KV_LORA + LANES] * cosp_ref[...] + lat[:, KV_LORA + LANES:] * sinp_ref[...]
    kn = jnp.dot(ckv_tok.astype(BF16), wkn_ref[...], preferred_element_type=F32)
    for h in range(N_HEADS):
        k_ref[0, h] = (kn[:, h * HEAD_PAD:(h + 1) * HEAD_PAD] + k_rope).astype(BF16)


def _mla_proj(x2d, tabs, w, batch, seq):
    n, d = x2d.shape
    t = TOK_TILE
    tiles_per_seq = seq // t
    q_scale = (QK_NOPE + QK_ROPE) ** -0.5 * math.log2(math.e)
    full = lambda a: pl.BlockSpec(a.shape, lambda i: (0,) * a.ndim)
    bs_idx = lambda i: (i // tiles_per_seq, 0, 0, i % tiles_per_seq)
    return pl.pallas_call(
        functools.partial(_mla_proj_kernel, q_scale=q_scale),
        grid=(n // t,),
        in_specs=[
            pl.BlockSpec((t, d), lambda i: (i, 0)),
            pl.BlockSpec((1, HALF_ROPE, t), lambda i: (i // tiles_per_seq, 0, i % tiles_per_seq)),
            pl.BlockSpec((1, HALF_ROPE, t), lambda i: (i // tiles_per_seq, 0, i % tiles_per_seq)),
            pl.BlockSpec((t, LANES), lambda i: (i, 0)),
            pl.BlockSpec((t, LANES), lambda i: (i, 0)),
            full(w['w_fm']), full(w['w_tok']), full(w['w_uq_t']), full(w['w_v_t']), full(w['w_kn']),
            full(w['qn_col']), full(w['kvn_col']), full(w['kvn_row']),
        ],
        out_specs=[
            pl.BlockSpec((1, N_HEADS, HEAD_PAD, t), bs_idx),
            pl.BlockSpec((1, N_HEADS, t, HEAD_PAD), lambda i: (i // tiles_per_seq, 0, i % tiles_per_seq, 0)),
            pl.BlockSpec((1, N_HEADS, V_ROWS, t), bs_idx),
        ],
        out_shape=[
            jax.ShapeDtypeStruct((batch, N_HEADS, HEAD_PAD, seq), BF16),
            jax.ShapeDtypeStruct((batch, N_HEADS, seq, HEAD_PAD), BF16),
            jax.ShapeDtypeStruct((batch, N_HEADS, V_ROWS, seq), BF16),
        ],
        compiler_params=pltpu.CompilerParams(dimension_semantics=("arbitrary",),
                                             vmem_limit_bytes=56 * 1024 * 1024),
        name="mla_proj",
    )(x2d, tabs['cos_t'], tabs['sin_t'], tabs['cos_pad'], tabs['sin_pad'],
      w['w_fm'], w['w_tok'], w['w_uq_t'], w['w_v_t'], w['w_kn'], w['qn_col'], w['kvn_col'], w['kvn_row'])


def _attn_kernel(qt_ref, k_ref, vt_ref, o_ref, s_scr, p_scr):
    n_h = qt_ref.shape[1]
    seq = k_ref.shape[2]
    tq, tk = ATT_TQ, ATT_TK
    n_q = seq // tq
    heads = range(n_h)

    def load_q(q0):
        return [qt_ref[0, h, :, pl.ds(q0, tq)] for h in heads]

    def scores(h, q_t, k0, slot):
        kb = k_ref[0, h, pl.ds(k0, tk), :]
        s = jnp.dot(kb, q_t, preferred_element_type=F32)
        s_scr[h, slot] = s
        return jnp.max(s, axis=0, keepdims=True)

    def softmax(h, slot, m, bmax):
        s = s_scr[h, slot]
        m_new = jnp.maximum(m, bmax)
        p_scr[h, slot] = jnp.exp2(s - m_new).astype(BF16)
        return m_new, jnp.exp2(m - m_new)

    def values(h, k0, slot, alpha, acc):
        vb = vt_ref[0, h, :, pl.ds(k0, tk)]
        return alpha * acc + jnp.dot(vb, p_scr[h, slot], preferred_element_type=F32)

    def trip(t, carry, q_ts):
        m, acc, a_pend, bmax0 = [list(c) for c in carry]
        ka = pl.multiple_of(t * tq, tq)
        k_prev = pl.multiple_of(jnp.maximum(ka - tk, 0), tk)
        a0, bmax1 = [None] * n_h, [None] * n_h
        for h in heads:
            bmax1[h] = scores(h, q_ts[h], ka + tk, 1)
        for h in heads:
            m[h], a0[h] = softmax(h, 0, m[h], bmax0[h])
        for h in heads:
            acc[h] = values(h, k_prev, 1, a_pend[h], acc[h])
        for h in heads:
            bmax0[h] = scores(h, q_ts[h], ka + tq, 0)
        for h in heads:
            m[h], a_pend[h] = softmax(h, 1, m[h], bmax1[h])
        for h in heads:
            acc[h] = values(h, ka, 0, a0[h], acc[h])
        return tuple(m), tuple(acc), tuple(a_pend), tuple(bmax0)

    def diagonal_trip(q0, carry, q_ts, q_next):
        m, acc, a_pend, _ = carry
        k_prev = pl.multiple_of(jnp.maximum(q0 - tk, 0), tk)
        tri = lax.broadcasted_iota(I32, (tk, tk), 0) <= lax.broadcasted_iota(I32, (tk, tk), 1)
        bmax_next = []
        for h in heads:
            kb = k_ref[0, h, pl.ds(q0 + tk, tk), :]
            s_b = jnp.dot(kb, q_ts[h][:, tk:], preferred_element_type=F32)
            s_a = s_scr[h, 0]
            s_al = jnp.where(tri, s_a[:, :tk], NEG_BIG)
            s_ar = s_a[:, tk:]
            bmax_a = jnp.concatenate([jnp.max(s_al, axis=0, keepdims=True),
                                      jnp.max(s_ar, axis=0, keepdims=True)], axis=1)
            m_a = jnp.maximum(m[h], bmax_a)
            a0 = jnp.exp2(m[h] - m_a)
            p_scr[h, 0] = jnp.concatenate([jnp.exp2(s_al - m_a[:, :tk]), jnp.exp2(s_ar - m_a[:, tk:])],
                                          axis=1).astype(BF16)
            acc_h = values(h, k_prev, 1, a_pend[h], acc[h])
            bmax_next.append(scores(h, q_next[h], 0, 0))
            s_b = jnp.where(tri, s_b, NEG_BIG)
            m_r = jnp.maximum(m_a[:, tk:], jnp.max(s_b, axis=0, keepdims=True))
            p_b = jnp.exp2(s_b - m_r).astype(BF16)
            acc_h = values(h, q0, 0, a0, acc_h)
            vb = vt_ref[0, h, :, pl.ds(q0 + tk, tk)]
            acc_r = jnp.exp2(m_a[:, tk:] - m_r) * acc_h[:, tk:] + jnp.dot(vb, p_b, preferred_element_type=F32)
            out = jnp.concatenate([acc_h[:, :tk], acc_r], axis=1)
            o_ref[0, h * V_DIM:(h + 1) * V_DIM, pl.ds(q0, tq)] = (
                out[0:V_DIM] / out[V_DIM:V_DIM + 1]).astype(BF16)
        return tuple(bmax_next)

    def q_tile(qi, bmax0):
        q0 = pl.multiple_of(qi * tq, tq)
        q_ts = load_q(q0)
        q_next = load_q(pl.multiple_of(jnp.minimum(q0 + tq, seq - tq), tq))
        for h in heads:
            p_scr[h, 1] = jnp.zeros((tk, tq), BF16)
        init = (tuple(jnp.full((1, tq), NEG_BIG, F32) for _ in heads),
                tuple(jnp.zeros((V_ROWS, tq), F32) for _ in heads),
                tuple(jnp.ones((1, tq), F32) for _ in heads),
                bmax0)
        res = lax.fori_loop(0, qi, lambda t, c: trip(t, c, q_ts), init)
        return diagonal_trip(q0, res, q_ts, q_next)

    q_first = load_q(0)
    lax.fori_loop(0, n_q, q_tile, tuple(scores(h, q_first[h], 0, 0) for h in heads))


def _attention(q_t, k, v_t):
    batch, heads, _, seq = q_t.shape
    g = ATT_HEADS_PER_STEP
    assert ATT_TQ == 2 * ATT_TK and seq % ATT_TQ == 0 and heads % g == 0
    return pl.pallas_call(
        _attn_kernel,
        grid=(batch, heads // g),
        in_specs=[
            pl.BlockSpec((1, g, HEAD_PAD, seq), lambda b, h: (b, h, 0, 0)),
            pl.BlockSpec((1, g, seq, HEAD_PAD), lambda b, h: (b, h, 0, 0)),
            pl.BlockSpec((1, g, V_ROWS, seq), lambda b, h: (b, h, 0, 0)),
        ],
        out_specs=pl.BlockSpec((1, g * V_DIM, seq), lambda b, h: (b, h, 0)),
        out_shape=jax.ShapeDtypeStruct((batch, heads * V_DIM, seq), BF16),
        scratch_shapes=[pltpu.VMEM((g, 2, ATT_TK, ATT_TQ), F32), pltpu.VMEM((g, 2, ATT_TK, ATT_TQ), BF16)],
        compiler_params=pltpu.CompilerParams(dimension_semantics=("arbitrary", "arbitrary"),
                                             vmem_limit_bytes=56 * 1024 * 1024),
        name="mla_attention",
    )(q_t, k, v_t)


def _outproj_ln_kernel(ot_ref, x_ref, wo_ref, g_ref, b_ref, out_ref, *, alpha):
    f = lax.dot_general(ot_ref[0], wo_ref[...], _TN, preferred_element_type=F32)
    out_ref[...] = _layer_norm(alpha * x_ref[...] + f, g_ref[...], b_ref[...])


def _outproj_ln(o_t, x2d, w_o, g, b, alpha):
    n, d = x2d.shape
    batch, hd, seq = o_t.shape
    t = LN_TILE
    tiles_per_seq = seq // t
    return pl.pallas_call(
        functools.partial(_outproj_ln_kernel, alpha=alpha),
        grid=(n // t,),
        in_specs=[
            pl.BlockSpec((1, hd, t), lambda i: (i // tiles_per_seq, 0, i % tiles_per_seq)),
            pl.BlockSpec((t, d), lambda i: (i, 0)),
            pl.BlockSpec((hd, d), lambda i: (0, 0)),
            pl.BlockSpec((1, d), lambda i: (0, 0)),
            pl.BlockSpec((1, d), lambda i: (0, 0)),
        ],
        out_specs=pl.BlockSpec((t, d), lambda i: (i, 0)),
        out_shape=jax.ShapeDtypeStruct((n, d), F32),
        compiler_params=pltpu.CompilerParams(dimension_semantics=("arbitrary",)),
        name="mla_outproj_ln",
    )(o_t, x2d, w_o, g, b)


def _pool_ln_kernel(x_ref, halo_ref, w_ref, pb_ref, ps_ref, g_ref, b_ref, out_ref, *, alpha, tiles_per_seq):
    t, d = x_ref.shape
    gd = d // len(POOL_WINDOWS)
    i = pl.program_id(0)
    tile_in_seq = i % tiles_per_seq
    x = x_ref[...]
    halo = jnp.where(tile_in_seq == 0, 0.0, halo_ref[...])
    ext = jnp.concatenate([halo, x], axis=0)
    pos = tile_in_seq * t + lax.broadcasted_iota(I32, (t, 1), 0)
    ys = []
    for gi, win in enumerate(POOL_WINDOWS):
        e = ext[:, gi * gd:(gi + 1) * gd]
        sh = 1
        while sh < win:
            e = e + pltpu.roll(e, sh, axis=0)
            sh *= 2
        inv_cnt = 1.0 / jnp.minimum(pos + 1, win).astype(F32)
        pooled = e[POOL_HALO:] * inv_cnt - x[:, gi * gd:(gi + 1) * gd]
        ys.append(jnp.dot(pooled.astype(BF16), w_ref[gi], preferred_element_type=F32))
    y = (jnp.concatenate(ys, axis=1) + pb_ref[...]) * ps_ref[...]
    out_ref[...] = _layer_norm(alpha * x + y, g_ref[...], b_ref[...])


def _pool_ln(x2d, seq, w, pb, ps, g, b, alpha):
    n, d = x2d.shape
    t = LN_TILE
    tiles_per_seq = seq // t
    halo_blocks = t // POOL_HALO
    vec = pl.BlockSpec((1, d), lambda i: (0, 0))
    return pl.pallas_call(
        functools.partial(_pool_ln_kernel, alpha=alpha, tiles_per_seq=tiles_per_seq),
        grid=(n // t,),
        in_specs=[
            pl.BlockSpec((t, d), lambda i: (i, 0)),
            pl.BlockSpec((POOL_HALO, d), lambda i: (jnp.maximum(i * halo_blocks - 1, 0), 0)),
            pl.BlockSpec(w.shape, lambda i: (0, 0, 0)),
            vec, vec, vec, vec,
        ],
        out_specs=pl.BlockSpec((t, d), lambda i: (i, 0)),
        out_shape=jax.ShapeDtypeStruct((n, d), F32),
        compiler_params=pltpu.CompilerParams(dimension_semantics=("arbitrary",)),
        name="pool_ln",
    )(x2d, x2d, w, pb, ps, g, b)


def _router_kernel(x_ref, whi_ref, wlo_ref, bias_ref, tri_ref, eid_ref, gate_ref, rank_ref, cnt_ref, run_ref):
    t = x_ref.shape[0]

    @pl.when(pl.program_id(0) == 0)
    def _():
        run_ref[...] = jnp.zeros_like(run_ref)

    x = x_ref[...]
    xh = x.astype(BF16)
    xl = (x - xh.astype(F32)).astype(BF16)
    lg = (lax.dot_general(whi_ref[...], xh, _NT, preferred_element_type=F32)
          + lax.dot_general(whi_ref[...], xl, _NT, preferred_element_type=F32)
          + lax.dot_general(wlo_ref[...], xh, _NT, preferred_element_type=F32)
          + bias_ref[...])
    ridx = lax.broadcasted_iota(I32, (N_GROUPS, t), 0)

    g = lg[0:N_GROUPS]
    gmax = jnp.max(g, axis=0, keepdims=True)
    g_sel = jnp.min(jnp.where(g == gmax, ridx, N_GROUPS), axis=0, keepdims=True)
    g_gate = 1.0 / jnp.sum(jnp.exp(g - gmax), axis=0, keepdims=True)

    e_all = lg[N_GROUPS:N_GROUPS + N_EXPERTS]
    e = jnp.zeros((EXP_PER_GROUP, t), F32)
    for gi in range(N_GROUPS):
        e = e + jnp.where(g_sel == gi, e_all[gi * EXP_PER_GROUP:(gi + 1) * EXP_PER_GROUP], 0.0)
    pe = jnp.exp(e - jnp.max(e, axis=0, keepdims=True))
    p1 = jnp.max(pe, axis=0, keepdims=True)
    i1 = jnp.min(jnp.where(pe == p1, ridx, EXP_PER_GROUP), axis=0, keepdims=True)
    pe2 = jnp.where(ridx == i1, -1.0, pe)
    p2 = jnp.max(pe2, axis=0, keepdims=True)
    i2 = jnp.min(jnp.where(pe2 == p2, ridx, EXP_PER_GROUP), axis=0, keepdims=True)
    denom = p1 + p2
    e1 = g_sel * EXP_PER_GROUP + i1
    e2 = g_sel * EXP_PER_GROUP + i2
    eid_ref[0:1, :] = e1
    eid_ref[1:2, :] = e2
    gate_ref[0:1, :] = g_gate * p1 / denom
    gate_ref[1:2, :] = g_gate * p2 / denom

    eidx = lax.broadcasted_iota(I32, (N_EXPERTS, t), 0)
    oh1 = eidx == e1
    oh2 = eidx == e2
    oh = jnp.where(oh1 | oh2, 1.0, 0.0)
    earlier = jnp.dot(oh.astype(BF16), tri_ref[...], preferred_element_type=F32)
    tot = earlier + run_ref[:, 0:1]
    rank_ref[0:1, :] = jnp.sum(jnp.where(oh1, tot, 0.0), axis=0, keepdims=True).astype(I32)
    rank_ref[1:2, :] = jnp.sum(jnp.where(oh2, tot, 0.0), axis=0, keepdims=True).astype(I32)
    run_ref[...] = run_ref[...] + jnp.sum(oh, axis=1, keepdims=True)
    cnt_ref[...] = run_ref[...].astype(I32)


def _router(x2d, w_hi, w_lo, bias, tri):
    n, d = x2d.shape
    t = ROUTE_TILE
    tok = lambda dt: jax.ShapeDtypeStruct((TOP_K, n), dt)
    tok_spec = pl.BlockSpec((TOP_K, t), lambda i: (0, i))
    return pl.pallas_call(
        _router_kernel,
        grid=(n // t,),
        in_specs=[
            pl.BlockSpec((t, d), lambda i: (i, 0)),
            pl.BlockSpec(w_hi.shape, lambda i: (0, 0)),
            pl.BlockSpec(w_lo.shape, lambda i: (0, 0)),
            pl.BlockSpec(bias.shape, lambda i: (0, 0)),
            pl.BlockSpec(tri.shape, lambda i: (0, 0)),
        ],
        out_specs=[tok_spec, tok_spec, tok_spec, pl.BlockSpec((N_EXPERTS, LANES), lambda i: (0, 0))],
        out_shape=[tok(I32), tok(F32), tok(I32), jax.ShapeDtypeStruct((N_EXPERTS, LANES), I32)],
        scratch_shapes=[pltpu.VMEM((N_EXPERTS, LANES), F32)],
        compiler_params=pltpu.CompilerParams(dimension_semantics=("arbitrary",)),
        name="moe_router",
    )(x2d, w_hi, w_lo, bias, tri)


ROWS_PER_TRIP = 8


def _pack_halves(y):
    half = y.shape[1] // 2
    bits = lax.bitcast_convert_type(y.astype(BF16).astype(F32), jnp.uint32)
    return bits[:, :half] | (bits[:, half:] >> 16)


def _unpack_halves(p):
    hi = lax.bitcast_convert_type(p & jnp.uint32(0xFFFF0000), F32)
    lo = lax.bitcast_convert_type(p << 16, F32)
    return hi, lo


def _for_row_groups(n_rows, body):
    def trip(g, c):
        base = g * ROWS_PER_TRIP
        for u in range(ROWS_PER_TRIP):
            body(g, u, base + u)
        return c
    lax.fori_loop(0, n_rows // ROWS_PER_TRIP, trip, 0)


def _staged_row(buf, g, u):
    return buf.at[g, pl.ds(u, 1)]


def _hbm_row(arr, row):
    return arr.at[pl.ds(row, 1)]


def _slots_kernel(pstart_ref, eid_ref, rank_ref, dest_ref):
    eid = eid_ref[...]

    def add_expert(e, acc):
        return acc + jnp.where(eid == e, pstart_ref[e], 0)

    dest_ref[...] = lax.fori_loop(0, N_EXPERTS, add_expert, rank_ref[...])


def _slots(pstart, eid, rank):
    full = pl.BlockSpec(eid.shape, lambda i, ps: (0, 0))
    return pl.pallas_call(
        _slots_kernel,
        grid_spec=pltpu.PrefetchScalarGridSpec(num_scalar_prefetch=1, grid=(1,),
                                               in_specs=[full, full], out_specs=full),
        out_shape=jax.ShapeDtypeStruct(eid.shape, I32),
        name="moe_slots",
    )(pstart, eid, rank)


def _dispatch_kernel(d0_ref, d1_ref, x_ref, rows_in_hbm, rows_hbm, xp_scr, sem):
    del rows_in_hbm
    t = x_ref.shape[0]
    xp_scr[...] = _pack_halves(x_ref[...]).reshape(xp_scr.shape)

    def issue(g, u, r):
        for k, d_ref in enumerate((d0_ref, d1_ref)):
            pltpu.make_async_copy(_staged_row(xp_scr, g, u), _hbm_row(rows_hbm, d_ref[r]), sem).start(priority=k)

    def drain(g, u, r):
        for _ in range(TOP_K):
            pltpu.make_async_copy(_staged_row(xp_scr, 0, 0), _hbm_row(rows_hbm, 0), sem).wait()

    _for_row_groups(t, issue)
    _for_row_groups(t, drain)


def _dispatch(dest, x2d, rows_init):
    n, d = x2d.shape
    t = DISPATCH_TILE
    tok_spec = pl.BlockSpec((t,), lambda i: (i,), memory_space=pltpu.SMEM)
    return pl.pallas_call(
        _dispatch_kernel,
        grid=(n // t,),
        in_specs=[
            tok_spec, tok_spec,
            pl.BlockSpec((t, d), lambda i: (i, 0)),
            pl.BlockSpec(memory_space=pl.ANY),
        ],
        out_specs=pl.BlockSpec(memory_space=pl.ANY),
        out_shape=jax.ShapeDtypeStruct(rows_init.shape, rows_init.dtype),
        scratch_shapes=[pltpu.VMEM((t // ROWS_PER_TRIP, ROWS_PER_TRIP, d // 2), jnp.uint32),
                        pltpu.SemaphoreType.DMA(())],
        input_output_aliases={3: 0},
        compiler_params=pltpu.CompilerParams(dimension_semantics=("arbitrary",)),
        name="moe_dispatch",
    )(dest[0], dest[1], x2d, rows_init)


def _ffn_kernel(be_ref, nu_ref, rows_ref, w1_ref, w3_ref, w2_ref, y_ref, w1b, w3b, w2b):
    b = pl.program_id(0)
    half = rows_ref.shape[1]

    @pl.when(b < nu_ref[0])
    def _():
        prev = be_ref[jnp.maximum(b - 1, 0)]

        @pl.when((b == 0) | (be_ref[b] != prev))
        def _():
            w1b[...] = w1_ref[0, 0].astype(BF16)
            w3b[...] = w3_ref[0, 0].astype(BF16)
            w2b[...] = w2_ref[0, 0].astype(BF16)

        for c in range(rows_ref.shape[0] // FFN_CHUNK):
            rows = pl.ds(c * FFN_CHUNK, FFN_CHUNK)
            x_hi, x_lo = _unpack_halves(rows_ref[rows, :])
            x_hi, x_lo = x_hi.astype(BF16), x_lo.astype(BF16)

            def up(w):
                return (jnp.dot(x_hi, w[0:half, :], preferred_element_type=F32)
                        + jnp.dot(x_lo, w[half:2 * half, :], preferred_element_type=F32))

            h1 = up(w1b)
            hb = h1 * jax.nn.sigmoid(h1) * up(w3b)
            y_ref[rows, :] = _pack_halves(jnp.dot(hb.astype(BF16), w2b[...], preferred_element_type=F32))


def _expert_ffn(blk_e, n_used, rows, w1, w3, w2, layer):
    r, half = rows.shape
    d, de = w1.shape[2], w1.shape[3]
    bm = EXPERT_BLOCK
    nb = r // bm
    clamp = lambda b, nu: jnp.minimum(b, nu[0] - 1)
    row_spec = pl.BlockSpec((bm, half), lambda b, be, nu: (clamp(b, nu), 0))
    w_idx = lambda b, be, nu: (layer, be[clamp(b, nu)], 0, 0)
    return pl.pallas_call(
        _ffn_kernel,
        grid_spec=pltpu.PrefetchScalarGridSpec(
            num_scalar_prefetch=2,
            grid=(nb,),
            in_specs=[
                row_spec,
                pl.BlockSpec((1, 1, d, de), w_idx),
                pl.BlockSpec((1, 1, d, de), w_idx),
                pl.BlockSpec((1, 1, de, d), w_idx),
            ],
            out_specs=row_spec,
            scratch_shapes=[pltpu.VMEM((d, de), BF16), pltpu.VMEM((d, de), BF16), pltpu.VMEM((de, d), BF16)],
        ),
        out_shape=jax.ShapeDtypeStruct((r, half), jnp.uint32),
        compiler_params=pltpu.CompilerParams(dimension_semantics=("arbitrary",)),
        name="moe_expert_ffn",
    )(blk_e, n_used, rows, w1, w3, w2)


def _combine_ln_kernel(d0_ref, d1_ref, n0_ref, n1_ref, gate_ref, x_ref, y_hbm, g_ref, b_ref, out_ref,
                       ybuf, sems, *, alpha):
    t = COMBINE_TILE
    i = pl.program_id(0)

    def request_row(slot, d_refs, row, g, u):
        for k, d_ref in enumerate(d_refs):
            pltpu.make_async_copy(_hbm_row(y_hbm, d_ref[row]), _staged_row(ybuf.at[slot, k], g, u),
                                  sems.at[slot]).start(priority=k)

    def wait_tile(slot):
        def wait(g, u, r):
            for k in range(TOP_K):
                pltpu.make_async_copy(_hbm_row(y_hbm, 0), _staged_row(ybuf.at[slot, k], 0, 0), sems.at[slot]).wait()
        _for_row_groups(t, wait)

    def request_tile(slot, d_refs, first_row):
        _for_row_groups(t, lambda g, u, r: request_row(slot, d_refs, first_row + r, g, u))

    def reduce_tile(slot, first_row):
        wait_tile(slot)
        rows = pl.ds(first_row, t)
        gate = gate_ref[rows, :]
        hi0, lo0 = _unpack_halves(ybuf[slot, 0].reshape(t, -1))
        hi1, lo1 = _unpack_halves(ybuf[slot, 1].reshape(t, -1))
        g0, g1 = gate[:, 0:1], gate[:, 1:2]
        f = jnp.concatenate([g0 * hi0 + g1 * hi1, g0 * lo0 + g1 * lo1], axis=1)
        out_ref[rows, :] = _layer_norm(alpha * x_ref[rows, :] + f, g_ref[...], b_ref[...])

    @pl.when(i == 0)
    def _():
        request_tile(0, (d0_ref, d1_ref), 0)

    request_tile(1, (d0_ref, d1_ref), t)
    reduce_tile(0, 0)

    @pl.when(i + 1 < pl.num_programs(0))
    def _():
        request_tile(0, (n0_ref, n1_ref), 0)

    reduce_tile(1, t)


def _combine_ln(dest, gates_tok, x2d, y_rows, g, b, alpha):
    n, d = x2d.shape
    t = COMBINE_TILE
    n_tiles = n // t
    vec = pl.BlockSpec((1, d), lambda i: (0, 0))
    pair_spec = pl.BlockSpec((2 * t,), lambda i: (i,), memory_space=pltpu.SMEM)
    next_spec = pl.BlockSpec((t,), lambda i: (jnp.minimum(2 * i + 2, n_tiles - 1),), memory_space=pltpu.SMEM)
    return pl.pallas_call(
        functools.partial(_combine_ln_kernel, alpha=alpha),
        grid=(n_tiles // 2,),
        in_specs=[
            pair_spec, pair_spec, next_spec, next_spec,
            pl.BlockSpec((2 * t, TOP_K), lambda i: (i, 0)),
            pl.BlockSpec((2 * t, d), lambda i: (i, 0)),
            pl.BlockSpec(memory_space=pl.ANY),
            vec, vec,
        ],
        out_specs=pl.BlockSpec((2 * t, d), lambda i: (i, 0)),
        out_shape=jax.ShapeDtypeStruct((n, d), F32),
        scratch_shapes=[pltpu.VMEM((2, TOP_K, t // ROWS_PER_TRIP, ROWS_PER_TRIP, d // 2), jnp.uint32),
                        pltpu.SemaphoreType.DMA((2,))],
        compiler_params=pltpu.CompilerParams(dimension_semantics=("arbitrary",)),
        name="moe_combine_ln",
    )(dest[0], dest[1], dest[0], dest[1], gates_tok, x2d, y_rows, g, b)


def _rope_tables(positions):
    inv_freq = ROPE_THETA ** (-jnp.arange(0, QK_ROPE, 2, dtype=F32) / QK_ROPE)
    ang = positions.astype(F32)[..., None] * inv_freq
    cos, sin = jnp.cos(ang), jnp.sin(ang)
    b, s, _ = cos.shape
    zeros = jnp.zeros((b * s, QK_NOPE), F32)
    tail = jnp.zeros((b * s, LANES - QK_NOPE - QK_ROPE), F32)
    c2, s2 = cos.reshape(b * s, -1), sin.reshape(b * s, -1)
    return {
        'cos_t': jnp.swapaxes(cos, 1, 2), 'sin_t': jnp.swapaxes(sin, 1, 2),
        'cos_pad': jnp.concatenate([zeros, c2, c2, tail], axis=1),
        'sin_pad': jnp.concatenate([zeros, s2, s2, tail], axis=1),
    }


def _mla_weights(w_in, q_norm, w_uq, kv_norm, w_ukv, w_o):
    d = w_in.shape[0]
    lat_q_kv = Q_LORA + KV_LORA
    r1 = w_in[:, lat_q_kv:lat_q_kv + HALF_ROPE]
    r2 = w_in[:, lat_q_kv + HALF_ROPE:lat_q_kv + QK_ROPE]
    z_lo = jnp.zeros((d, QK_NOPE), F32)
    z_hi = jnp.zeros((d, LANES - QK_NOPE - QK_ROPE), F32)
    w_tok = jnp.concatenate([w_in[:, Q_LORA:lat_q_kv],
                             z_lo, r1, r2, z_hi,
                             z_lo, -r2, r1, z_hi],
                            axis=1)
    uq = w_uq.reshape(Q_LORA, N_HEADS, QK_NOPE + QK_ROPE)
    w_uq_t = jnp.concatenate([uq[:, :, :QK_NOPE].reshape(Q_LORA, -1),
                              uq[:, :, QK_NOPE:QK_NOPE + HALF_ROPE].reshape(Q_LORA, -1),
                              uq[:, :, QK_NOPE + HALF_ROPE:].reshape(Q_LORA, -1)], axis=1).T
    ukv = w_ukv.reshape(KV_LORA, N_HEADS, QK_NOPE + V_DIM)
    w_kn = jnp.concatenate([ukv[:, :, :QK_NOPE], jnp.zeros((KV_LORA, N_HEADS, HEAD_PAD - QK_NOPE), F32)],
                           axis=2).reshape(KV_LORA, N_HEADS * HEAD_PAD)
    w_v_t = ukv[:, :, QK_NOPE:].reshape(KV_LORA, N_HEADS * V_DIM).T
    return {
        'w_fm': w_in[:, :lat_q_kv].T.astype(BF16), 'w_tok': w_tok.astype(BF16),
        'w_uq_t': w_uq_t.astype(BF16), 'w_v_t': w_v_t.astype(BF16), 'w_kn': w_kn.astype(BF16),
        'qn_col': q_norm.reshape(-1, 1), 'kvn_col': kv_norm.reshape(-1, 1), 'kvn_row': kv_norm.reshape(1, -1),
        'w_o': w_o.astype(BF16),
    }


def _router_weights(w_grp, b_grp, w_exp, b_exp):
    d = w_grp.shape[0]
    pad = LANES - N_GROUPS - N_EXPERTS
    w = jnp.concatenate([w_grp, w_exp, jnp.zeros((d, pad), F32)], axis=1).T
    w_hi = w.astype(BF16)
    w_lo = (w - w_hi.astype(F32)).astype(BF16)
    bias = jnp.concatenate([b_grp, b_exp, jnp.zeros((pad,), F32)]).reshape(-1, 1)
    return w_hi, w_lo, bias


def _moe_ln(x2d, w_hi, w_lo, bias, tri, w1, w3, w2, layer, g, b, alpha, rows_buf):
    n, d = x2d.shape
    bm = EXPERT_BLOCK
    n_assign = n * TOP_K
    nb = -(-(n_assign + N_EXPERTS * (bm - 1)) // bm)
    if rows_buf is None:
        rows_buf = jnp.zeros((nb * bm, d // 2), jnp.uint32)
    eid, gates, rank, cnt = _router(x2d, w_hi, w_lo, bias, tri)
    counts = cnt[:, 0]
    padded = (counts + bm - 1) // bm * bm
    pend = jnp.cumsum(padded)
    pstart = (pend - padded).astype(I32)
    n_used = (pend[-1:] // bm).astype(I32)
    blk_start = jnp.arange(nb, dtype=I32) * bm
    blk_e = jnp.minimum(jnp.sum((pend[None, :] <= blk_start[:, None]).astype(I32), axis=1), N_EXPERTS - 1)
    dest = _slots(pstart, eid, rank)
    rows = _dispatch(dest, x2d, rows_buf)
    y_rows = _expert_ffn(blk_e, n_used, rows, w1, w3, w2, layer)
    return _combine_ln(dest, gates.T, x2d, y_rows, g, b, alpha), rows


def kernel(x, positions, ln_g, ln_b, mla_w_in, mla_q_norm, mla_w_uq, mla_kv_norm, mla_w_ukv, mla_w_o,
           pool_w, pool_b, pool_scale, moe_w_grp, moe_b_grp, moe_w_exp, moe_b_exp, moe_w1, moe_w3, moe_w2):
    batch, seq, d = x.shape
    depth = ln_g.shape[0]
    alpha = (2 * depth) ** 0.25
    tabs = _rope_tables(positions)
    t = ROUTE_TILE
    tri = (lax.broadcasted_iota(I32, (t, t), 0) < lax.broadcasted_iota(I32, (t, t), 1)).astype(BF16)
    x2d = x.reshape(batch * seq, d)
    vec = lambda v: v.reshape(1, d)
    rows_buf = None
    for i in range(depth):
        j = i // 2
        if i % 2 == 0:
            w = _mla_weights(mla_w_in[j], mla_q_norm[j], mla_w_uq[j], mla_kv_norm[j], mla_w_ukv[j], mla_w_o[j])
            q_t, k, v_t = _mla_proj(x2d, tabs, w, batch, seq)
            o_t = _attention(q_t, k, v_t)
            x2d = _outproj_ln(o_t, x2d, w['w_o'], vec(ln_g[i, 0]), vec(ln_b[i, 0]), alpha)
        else:
            x2d = _pool_ln(x2d, seq, pool_w[j].astype(BF16), vec(pool_b[j]), vec(pool_scale[j]),
                           vec(ln_g[i, 0]), vec(ln_b[i, 0]), alpha)
        w_hi, w_lo, bias = _router_weights(moe_w_grp[i], moe_b_grp[i], moe_w_exp[i], moe_b_exp[i])
        x2d, rows_buf = _moe_ln(x2d, w_hi, w_lo, bias, tri, moe_w1, moe_w3, moe_w2, i,
                                vec(ln_g[i, 1]), vec(ln_b[i, 1]), alpha, rows_buf)
    return x2d.reshape(batch, seq, d)
```

```python
import functools
import math

import jax
import jax.numpy as jnp
from jax import lax
from jax.experimental import pallas as pl
from jax.experimental.pallas import tpu as pltpu

F32 = jnp.float32
BF16 = jnp.bfloat16
I32 = jnp.int32

N_HEADS = 16
QK_NOPE = 64
QK_ROPE = 32
HALF_ROPE = QK_ROPE // 2
V_DIM = 64
Q_LORA = 256
KV_LORA = 128
ROPE_THETA = 10000.0
POOL_WINDOWS = (2, 4, 8, 16)
N_GROUPS = 8
EXP_PER_GROUP = 8
N_EXPERTS = N_GROUPS * EXP_PER_GROUP
TOP_K = 2
LN_EPS = 1e-5
RMS_EPS = 1e-6

LANES = 128
HEAD_PAD = 128
V_ROWS = 80
POOL_HALO = 16

TOK_TILE = 512
LN_TILE = 1024
ATT_TQ = 1024
ATT_TK = 512
ATT_HEADS_PER_STEP = 1
ROUTE_TILE = 512
DISPATCH_TILE = 1024
COMBINE_TILE = 256
EXPERT_BLOCK = 512
FFN_CHUNK = 256

NEG_BIG = -1e30
_NT = (((1,), (1,)), ((), ()))
_TN = (((0,), (0,)), ((), ()))


def _layer_norm(y, g, b):
    mu = jnp.mean(y, axis=-1, keepdims=True)
    yc = y - mu
    var = jnp.mean(yc * yc, axis=-1, keepdims=True)
    return yc * lax.rsqrt(var + LN_EPS) * g + b


def _mla_proj_kernel(x_ref, cost_ref, sint_ref, cosp_ref, sinp_ref, wfm_ref, wtok_ref, wuq_ref,
                     wv_ref, wkn_ref, qn_ref, kvnc_ref, kvnr_ref, qt_ref, k_ref, vt_ref, *, q_scale):
    t = x_ref.shape[0]
    xb = x_ref[...].astype(BF16)
    lat_t = lax.dot_general(wfm_ref[...], xb, _NT, preferred_element_type=F32)
    cq = lat_t[0:Q_LORA]
    cq = cq * lax.rsqrt(jnp.mean(cq * cq, axis=0, keepdims=True) + RMS_EPS) * qn_ref[...]
    ckv = lat_t[Q_LORA:Q_LORA + KV_LORA]
    ckv = ckv * lax.rsqrt(jnp.mean(ckv * ckv, axis=0, keepdims=True) + RMS_EPS) * kvnc_ref[...]

    q_t = jnp.dot(wuq_ref[...], cq.astype(BF16), preferred_element_type=F32) * q_scale
    n_nope = N_HEADS * QK_NOPE
    n_half = N_HEADS * HALF_ROPE
    qn = q_t[0:n_nope].reshape(N_HEADS, QK_NOPE, t)
    x1 = q_t[n_nope:n_nope + n_half].reshape(N_HEADS, HALF_ROPE, t)
    x2 = q_t[n_nope + n_half:n_nope + 2 * n_half].reshape(N_HEADS, HALF_ROPE, t)
    c = cost_ref[...]
    s = sint_ref[...]
    qt_ref[0, :, 0:QK_NOPE, :] = qn.astype(BF16)
    qt_ref[0, :, QK_NOPE:QK_NOPE + HALF_ROPE, :] = (x1 * c - x2 * s).astype(BF16)
    qt_ref[0, :, QK_NOPE + HALF_ROPE:QK_NOPE + QK_ROPE, :] = (x2 * c + x1 * s).astype(BF16)
    qt_ref[0, :, QK_NOPE + QK_ROPE:HEAD_PAD, :] = jnp.zeros(
        (N_HEADS, HEAD_PAD - QK_NOPE - QK_ROPE, t), BF16)

    v_t = jnp.dot(wv_ref[...], ckv.astype(BF16), preferred_element_type=F32)
    vt_ref[0, :, 0:V_DIM, :] = v_t.reshape(N_HEADS, V_DIM, t).astype(BF16)
    vt_ref[0, :, V_DIM:V_ROWS, :] = jnp.ones((N_HEADS, V_ROWS - V_DIM, t), BF16)

    lat = jnp.dot(xb, wtok_ref[...], preferred_element_type=F32)
    ckv_tok = lat[:, 0:KV_LORA]
    ckv_tok = ckv_tok * lax.rsqrt(jnp.mean(ckv_tok * ckv_tok, axis=-1, keepdims=True) + RMS_EPS) * kvnr_ref[...]
    k_rope = lat[:, KV_LORA:KV_LORA + LANES] * cosp_ref[...] + lat[:, KV_LORA + LANES:] * sinp_ref[...]
    kn = jnp.dot(ckv_tok.astype(BF16), wkn_ref[...], preferred_element_type=F32)
    for h in range(N_HEADS):
        k_ref[0, h] = (kn[:, h * HEAD_PAD:(h + 1) * HEAD_PAD] + k_rope).astype(BF16)


def _mla_proj(x2d, tabs, w, batch, seq):
    n, d = x2d.shape
    t = TOK_TILE
    tiles_per_seq = seq // t
    q_scale = (QK_NOPE + QK_ROPE) ** -0.5 * math.log2(math.e)
    full = lambda a: pl.BlockSpec(a.shape, lambda i: (0,) * a.ndim)
    bs_idx = lambda i: (i // tiles_per_seq, 0, 0, i % tiles_per_seq)
    return pl.pallas_call(
        functools.partial(_mla_proj_kernel, q_scale=q_scale),
        grid=(n // t,),
        in_specs=[
            pl.BlockSpec((t, d), lambda i: (i, 0)),
            pl.BlockSpec((1, HALF_ROPE, t), lambda i: (i // tiles_per_seq, 0, i % tiles_per_seq)),
            pl.BlockSpec((1, HALF_ROPE, t), lambda i: (i // tiles_per_seq, 0, i % tiles_per_seq)),
            pl.BlockSpec((t, LANES), lambda i: (i, 0)),
            pl.BlockSpec((t, LANES), lambda i: (i, 0)),
            full(w['w_fm']), full(w['w_tok']), full(w['w_uq_t']), full(w['w_v_t']), full(w['w_kn']),
            full(w['qn_col']), full(w['kvn_col']), full(w['kvn_row']),
        ],
        out_specs=[
            pl.BlockSpec((1, N_HEADS, HEAD_PAD, t), bs_idx),
            pl.BlockSpec((1, N_HEADS, t, HEAD_PAD), lambda i: (i // tiles_per_seq, 0, i % tiles_per_seq, 0)),
            pl.BlockSpec((1, N_HEADS, V_ROWS, t), bs_idx),
        ],
        out_shape=[
            jax.ShapeDtypeStruct((batch, N_HEADS, HEAD_PAD, seq), BF16),
            jax.ShapeDtypeStruct((batch, N_HEADS, seq, HEAD_PAD), BF16),
            jax.ShapeDtypeStruct((batch, N_HEADS, V_ROWS, seq), BF16),
        ],
        compiler_params=pltpu.CompilerParams(dimension_semantics=("arbitrary",),
                                             vmem_limit_bytes=56 * 1024 * 1024),
        name="mla_proj",
    )(x2d, tabs['cos_t'], tabs['sin_t'], tabs['cos_pad'], tabs['sin_pad'],
      w['w_fm'], w['w_tok'], w['w_uq_t'], w['w_v_t'], w['w_kn'], w['qn_col'], w['kvn_col'], w['kvn_row'])


def _attn_kernel(qt_ref, k_ref, vt_ref, o_ref, s_scr, p_scr):
    n_h = qt_ref.shape[1]
    seq = k_ref.shape[2]
    tq, tk = ATT_TQ, ATT_TK
    n_q = seq // tq
    heads = range(n_h)

    def load_q(q0):
        return [qt_ref[0, h, :, pl.ds(q0, tq)] for h in heads]

    def scores(h, q_t, k0, slot):
        kb = k_ref[0, h, pl.ds(k0, tk), :]
        s = jnp.dot(kb, q_t, preferred_element_type=F32)
        s_scr[h, slot] = s
        return jnp.max(s, axis=0, keepdims=True)

    def softmax(h, slot, m, bmax):
        s = s_scr[h, slot]
        m_new = jnp.maximum(m, bmax)
        p_scr[h, slot] = jnp.exp2(s - m_new).astype(BF16)
        return m_new, jnp.exp2(m - m_new)

    def values(h, k0, slot, alpha, acc):
        vb = vt_ref[0, h, :, pl.ds(k0, tk)]
        return alpha * acc + jnp.dot(vb, p_scr[h, slot], preferred_element_type=F32)

    def trip(t, carry, q_ts):
        m, acc, a_pend, bmax0 = [list(c) for c in carry]
        ka = pl.multiple_of(t * tq, tq)
        k_prev = pl.multiple_of(jnp.maximum(ka - tk, 0), tk)
        a0, bmax1 = [None] * n_h, [None] * n_h
        for h in heads:
            bmax1[h] = scores(h, q_ts[h], ka + tk, 1)
        for h in heads:
            m[h], a0[h] = softmax(h, 0, m[h], bmax0[h])
        for h in heads:
            acc[h] = values(h, k_prev, 1, a_pend[h], acc[h])
        for h in heads:
            bmax0[h] = scores(h, q_ts[h], ka + tq, 0)
        for h in heads:
            m[h], a_pend[h] = softmax(h, 1, m[h], bmax1[h])
        for h in heads:
            acc[h] = values(h, ka, 0, a0[h], acc[h])
        return tuple(m), tuple(acc), tuple(a_pend), tuple(bmax0)

    def diagonal_trip(q0, carry, q_ts, q_next):
        m, acc, a_pend, _ = carry
        k_prev = pl.multiple_of(jnp.maximum(q0 - tk, 0), tk)
        tri = lax.broadcasted_iota(I32, (tk, tk), 0) <= lax.broadcasted_iota(I32, (tk, tk), 1)
        bmax_next = []
        for h in heads:
            kb = k_ref[0, h, pl.ds(q0 + tk, tk), :]
            s_b = jnp.dot(kb, q_ts[h][:, tk:], preferred_element_type=F32)
            s_a = s_scr[h, 0]
            s_al = jnp.where(tri, s_a[:, :tk], NEG_BIG)
            s_ar = s_a[:, tk:]
            bmax_a = jnp.concatenate([jnp.max(s_al, axis=0, keepdims=True),
                                      jnp.max(s_ar, axis=0, keepdims=True)], axis=1)
            m_a = jnp.maximum(m[h], bmax_a)
            a0 = jnp.exp2(m[h] - m_a)
            p_scr[h, 0] = jnp.concatenate([jnp.exp2(s_al - m_a[:, :tk]), jnp.exp2(s_ar - m_a[:, tk:])],
                                          axis=1).astype(BF16)
            acc_h = values(h, k_prev, 1, a_pend[h], acc[h])
            bmax_next.append(scores(h, q_next[h], 0, 0))
            s_b = jnp.where(tri, s_b, NEG_BIG)
            m_r = jnp.maximum(m_a[:, tk:], jnp.max(s_b, axis=0, keepdims=True))
            p_b = jnp.exp2(s_b - m_r).astype(BF16)
            acc_h = values(h, q0, 0, a0, acc_h)
            vb = vt_ref[0, h, :, pl.ds(q0 + tk, tk)]
            acc_r = jnp.exp2(m_a[:, tk:] - m_r) * acc_h[:, tk:] + jnp.dot(vb, p_b, preferred_element_type=F32)
            out = jnp.concatenate([acc_h[:, :tk], acc_r], axis=1)
            o_ref[0, h * V_DIM:(h + 1) * V_DIM, pl.ds(q0, tq)] = (
                out[0:V_DIM] / out[V_DIM:V_DIM + 1]).astype(BF16)
        return tuple(bmax_next)

    def q_tile(qi, bmax0):
        q0 = pl.multiple_of(qi * tq, tq)
        q_ts = load_q(q0)
        q_next = load_q(pl.multiple_of(jnp.minimum(q0 + tq, seq - tq), tq))
        for h in heads:
            p_scr[h, 1] = jnp.zeros((tk, tq), BF16)
        init = (tuple(jnp.full((1, tq), NEG_BIG, F32) for _ in heads),
                tuple(jnp.zeros((V_ROWS, tq), F32) for _ in heads),
                tuple(jnp.ones((1, tq), F32) for _ in heads),
                bmax0)
        res = lax.fori_loop(0, qi, lambda t, c: trip(t, c, q_ts), init)
        return diagonal_trip(q0, res, q_ts, q_next)

    q_first = load_q(0)
    lax.fori_loop(0, n_q, q_tile, tuple(scores(h, q_first[h], 0, 0) for h in heads))


def _attention(q_t, k, v_t):
    batch, heads, _, seq = q_t.shape
    g = ATT_HEADS_PER_STEP
    assert ATT_TQ == 2 * ATT_TK and seq % ATT_TQ == 0 and heads % g == 0
    return pl.pallas_call(
        _attn_kernel,
        grid=(batch, heads // g),
        in_specs=[
            pl.BlockSpec((1, g, HEAD_PAD, seq), lambda b, h: (b, h, 0, 0)),
            pl.BlockSpec((1, g, seq, HEAD_PAD), lambda b, h: (b, h, 0, 0)),
            pl.BlockSpec((1, g, V_ROWS, seq), lambda b, h: (b, h, 0, 0)),
        ],
        out_specs=pl.BlockSpec((1, g * V_DIM, seq), lambda b, h: (b, h, 0)),
        out_shape=jax.ShapeDtypeStruct((batch, heads * V_DIM, seq), BF16),
        scratch_shapes=[pltpu.VMEM((g, 2, ATT_TK, ATT_TQ), F32), pltpu.VMEM((g, 2, ATT_TK, ATT_TQ), BF16)],
        compiler_params=pltpu.CompilerParams(dimension_semantics=("arbitrary", "arbitrary"),
                                             vmem_limit_bytes=56 * 1024 * 1024),
        name="mla_attention",
    )(q_t, k, v_t)


def _outproj_ln_kernel(ot_ref, x_ref, wo_ref, g_ref, b_ref, out_ref, *, alpha):
    f = lax.dot_general(ot_ref[0], wo_ref[...], _TN, preferred_element_type=F32)
    out_ref[...] = _layer_norm(alpha * x_ref[...] + f, g_ref[...], b_ref[...])


def _outproj_ln(o_t, x2d, w_o, g, b, alpha):
    n, d = x2d.shape
    batch, hd, seq = o_t.shape
    t = LN_TILE
    tiles_per_seq = seq // t
    return pl.pallas_call(
        functools.partial(_outproj_ln_kernel, alpha=alpha),
        grid=(n // t,),
        in_specs=[
            pl.BlockSpec((1, hd, t), lambda i: (i // tiles_per_seq, 0, i % tiles_per_seq)),
            pl.BlockSpec((t, d), lambda i: (i, 0)),
            pl.BlockSpec((hd, d), lambda i: (0, 0)),
            pl.BlockSpec((1, d), lambda i: (0, 0)),
            pl.BlockSpec((1, d), lambda i: (0, 0)),
        ],
        out_specs=pl.BlockSpec((t, d), lambda i: (i, 0)),
        out_shape=jax.ShapeDtypeStruct((n, d), F32),
        compiler_params=pltpu.CompilerParams(dimension_semantics=("arbitrary",)),
        name="mla_outproj_ln",
    )(o_t, x2d, w_o, g, b)


def _pool_ln_kernel(x_ref, halo_ref, w_ref, pb_ref, ps_ref, g_ref, b_ref, out_ref, *, alpha, tiles_per_seq):
    t, d = x_ref.shape
    gd = d // len(POOL_WINDOWS)
    i = pl.program_id(0)
    tile_in_seq = i % tiles_per_seq
    x = x_ref[...]
    halo = jnp.where(tile_in_seq == 0, 0.0, halo_ref[...])
    ext = jnp.concatenate([halo, x], axis=0)
    pos = tile_in_seq * t + lax.broadcasted_iota(I32, (t, 1), 0)
    ys = []
    for gi, win in enumerate(POOL_WINDOWS):
        e = ext[:, gi * gd:(gi + 1) * gd]
        sh = 1
        while sh < win:
            e = e + pltpu.roll(e, sh, axis=0)
            sh *= 2
        inv_cnt = 1.0 / jnp.minimum(pos + 1, win).astype(F32)
        pooled = e[POOL_HALO:] * inv_cnt - x[:, gi * gd:(gi + 1) * gd]
        ys.append(jnp.dot(pooled.astype(BF16), w_ref[gi], preferred_element_type=F32))
    y = (jnp.concatenate(ys, axis=1) + pb_ref[...]) * ps_ref[...]
    out_ref[...] = _layer_norm(alpha * x + y, g_ref[...], b_ref[...])


def _pool_ln(x2d, seq, w, pb, ps, g, b, alpha):
    n, d = x2d.shape
    t = LN_TILE
    tiles_per_seq = seq // t
    halo_blocks = t // POOL_HALO
    vec = pl.BlockSpec((1, d), lambda i: (0, 0))
    return pl.pallas_call(
        functools.partial(_pool_ln_kernel, alpha=alpha, tiles_per_seq=tiles_per_seq),
        grid=(n // t,),
        in_specs=[
            pl.BlockSpec((t, d), lambda i: (i, 0)),
            pl.BlockSpec((POOL_HALO, d), lambda i: (jnp.maximum(i * halo_blocks - 1, 0), 0)),
            pl.BlockSpec(w.shape, lambda i: (0, 0, 0)),
            vec, vec, vec, vec,
        ],
        out_specs=pl.BlockSpec((t, d), lambda i: (i, 0)),
        out_shape=jax.ShapeDtypeStruct((n, d), F32),
        compiler_params=pltpu.CompilerParams(dimension_semantics=("arbitrary",)),
        name="pool_ln",
    )(x2d, x2d, w, pb, ps, g, b)


def _router_kernel(x_ref, whi_ref, wlo_ref, bias_ref, tri_ref, eid_ref, gate_ref, rank_ref, cnt_ref, run_ref):
    t = x_ref.shape[0]

    @pl.when(pl.program_id(0) == 0)
    def _():
        run_ref[...] = jnp.zeros_like(run_ref)

    x = x_ref[...]
    xh = x.astype(BF16)
    xl = (x - xh.astype(F32)).astype(BF16)
    lg = (lax.dot_general(whi_ref[...], xh, _NT, preferred_element_type=F32)
          + lax.dot_general(whi_ref[...], xl, _NT, preferred_element_type=F32)
          + lax.dot_general(wlo_ref[...], xh, _NT, preferred_element_type=F32)
          + bias_ref[...])
    ridx = lax.broadcasted_iota(I32, (N_GROUPS, t), 0)

    g = lg[0:N_GROUPS]
    gmax = jnp.max(g, axis=0, keepdims=True)
    g_sel = jnp.min(jnp.where(g == gmax, ridx, N_GROUPS), axis=0, keepdims=True)
    g_gate = 1.0 / jnp.sum(jnp.exp(g - gmax), axis=0, keepdims=True)

    e_all = lg[N_GROUPS:N_GROUPS + N_EXPERTS]
    e = jnp.zeros((EXP_PER_GROUP, t), F32)
    for gi in range(N_GROUPS):
        e = e + jnp.where(g_sel == gi, e_all[gi * EXP_PER_GROUP:(gi + 1) * EXP_PER_GROUP], 0.0)
    pe = jnp.exp(e - jnp.max(e, axis=0, keepdims=True))
    p1 = jnp.max(pe, axis=0, keepdims=True)
    i1 = jnp.min(jnp.where(pe == p1, ridx, EXP_PER_GROUP), axis=0, keepdims=True)
    pe2 = jnp.where(ridx == i1, -1.0, pe)
    p2 = jnp.max(pe2, axis=0, keepdims=True)
    i2 = jnp.min(jnp.where(pe2 == p2, ridx, EXP_PER_GROUP), axis=0, keepdims=True)
    denom = p1 + p2
    e1 = g_sel * EXP_PER_GROUP + i1
    e2 = g_sel * EXP_PER_GROUP + i2
    eid_ref[0:1, :] = e1
    eid_ref[1:2, :] = e2
    gate_ref[0:1, :] = g_gate * p1 / denom
    gate_ref[1:2, :] = g_gate * p2 / denom

    eidx = lax.broadcasted_iota(I32, (N_EXPERTS, t), 0)
    oh1 = eidx == e1
    oh2 = eidx == e2
    oh = jnp.where(oh1 | oh2, 1.0, 0.0)
    earlier = jnp.dot(oh.astype(BF16), tri_ref[...], preferred_element_type=F32)
    tot = earlier + run_ref[:, 0:1]
    rank_ref[0:1, :] = jnp.sum(jnp.where(oh1, tot, 0.0), axis=0, keepdims=True).astype(I32)
    rank_ref[1:2, :] = jnp.sum(jnp.where(oh2, tot, 0.0), axis=0, keepdims=True).astype(I32)
    run_ref[...] = run_ref[...] + jnp.sum(oh, axis=1, keepdims=True)
    cnt_ref[...] = run_ref[...].astype(I32)


def _router(x2d, w_hi, w_lo, bias, tri):
    n, d = x2d.shape
    t = ROUTE_TILE
    tok = lambda dt: jax.ShapeDtypeStruct((TOP_K, n), dt)
    tok_spec = pl.BlockSpec((TOP_K, t), lambda i: (0, i))
    return pl.pallas_call(
        _router_kernel,
        grid=(n // t,),
        in_specs=[
            pl.BlockSpec((t, d), lambda i: (i, 0)),
            pl.BlockSpec(w_hi.shape, lambda i: (0, 0)),
            pl.BlockSpec(w_lo.shape, lambda i: (0, 0)),
            pl.BlockSpec(bias.shape, lambda i: (0, 0)),
            pl.BlockSpec(tri.shape, lambda i: (0, 0)),
        ],
        out_specs=[tok_spec, tok_spec, tok_spec, pl.BlockSpec((N_EXPERTS, LANES), lambda i: (0, 0))],
        out_shape=[tok(I32), tok(F32), tok(I32), jax.ShapeDtypeStruct((N_EXPERTS, LANES), I32)],
        scratch_shapes=[pltpu.VMEM((N_EXPERTS, LANES), F32)],
        compiler_params=pltpu.CompilerParams(dimension_semantics=("arbitrary",)),
        name="moe_router",
    )(x2d, w_hi, w_lo, bias, tri)


ROWS_PER_TRIP = 8


def _pack_halves(y):
    half = y.shape[1] // 2
    bits = lax.bitcast_convert_type(y.astype(BF16).astype(F32), jnp.uint32)
    return bits[:, :half] | (bits[:, half:] >> 16)


def _unpack_halves(p):
    hi = lax.bitcast_convert_type(p & jnp.uint32(0xFFFF0000), F32)
    lo = lax.bitcast_convert_type(p << 16, F32)
    return hi, lo


def _for_row_groups(n_rows, body):
    def trip(g, c):
        base = g * ROWS_PER_TRIP
        for u in range(ROWS_PER_TRIP):
            body(g, u, base + u)
        return c
    lax.fori_loop(0, n_rows // ROWS_PER_TRIP, trip, 0)


def _staged_row(buf, g, u):
    return buf.at[g, pl.ds(u, 1)]


def _hbm_row(arr, row):
    return arr.at[pl.ds(row, 1)]


def _slots_kernel(pstart_ref, eid_ref, rank_ref, dest_ref):
    eid = eid_ref[...]

    def add_expert(e, acc):
        return acc + jnp.where(eid == e, pstart_ref[e], 0)

    dest_ref[...] = lax.fori_loop(0, N_EXPERTS, add_expert, rank_ref[...])


def _slots(pstart, eid, rank):
    full = pl.BlockSpec(eid.shape, lambda i, ps: (0, 0))
    return pl.pallas_call(
        _slots_kernel,
        grid_spec=pltpu.PrefetchScalarGridSpec(num_scalar_prefetch=1, grid=(1,),
                                               in_specs=[full, full], out_specs=full),
        out_shape=jax.ShapeDtypeStruct(eid.shape, I32),
        name="moe_slots",
    )(pstart, eid, rank)


def _dispatch_kernel(d0_ref, d1_ref, x_ref, rows_in_hbm, rows_hbm, xp_scr, sem):
    del rows_in_hbm
    t = x_ref.shape[0]
    xp_scr[...] = _pack_halves(x_ref[...]).reshape(xp_scr.shape)

    def issue(g, u, r):
        for k, d_ref in enumerate((d0_ref, d1_ref)):
            pltpu.make_async_copy(_staged_row(xp_scr, g, u), _hbm_row(rows_hbm, d_ref[r]), sem).start(priority=k)

    def drain(g, u, r):
        for _ in range(TOP_K):
            pltpu.make_async_copy(_staged_row(xp_scr, 0, 0), _hbm_row(rows_hbm, 0), sem).wait()

    _for_row_groups(t, issue)
    _for_row_groups(t, drain)


def _dispatch(dest, x2d, rows_init):
    n, d = x2d.shape
    t = DISPATCH_TILE
    tok_spec = pl.BlockSpec((t,), lambda i: (i,), memory_space=pltpu.SMEM)
    return pl.pallas_call(
        _dispatch_kernel,
        grid=(n // t,),
        in_specs=[
            tok_spec, tok_spec,
            pl.BlockSpec((t, d), lambda i: (i, 0)),
            pl.BlockSpec(memory_space=pl.ANY),
        ],
        out_specs=pl.BlockSpec(memory_space=pl.ANY),
        out_shape=jax.ShapeDtypeStruct(rows_init.shape, rows_init.dtype),
        scratch_shapes=[pltpu.VMEM((t // ROWS_PER_TRIP, ROWS_PER_TRIP, d // 2), jnp.uint32),
                        pltpu.SemaphoreType.DMA(())],
        input_output_aliases={3: 0},
        compiler_params=pltpu.CompilerParams(dimension_semantics=("arbitrary",)),
        name="moe_dispatch",
    )(dest[0], dest[1], x2d, rows_init)


def _ffn_kernel(be_ref, nu_ref, rows_ref, w1_ref, w3_ref, w2_ref, y_ref, w1b, w3b, w2b):
    b = pl.program_id(0)
    half = rows_ref.shape[1]

    @pl.when(b < nu_ref[0])
    def _():
        prev = be_ref[jnp.maximum(b - 1, 0)]

        @pl.when((b == 0) | (be_ref[b] != prev))
        def _():
            w1b[...] = w1_ref[0, 0].astype(BF16)
            w3b[...] = w3_ref[0, 0].astype(BF16)
            w2b[...] = w2_ref[0, 0].astype(BF16)

        for c in range(rows_ref.shape[0] // FFN_CHUNK):
            rows = pl.ds(c * FFN_CHUNK, FFN_CHUNK)
            x_hi, x_lo = _unpack_halves(rows_ref[rows, :])
            x_hi, x_lo = x_hi.astype(BF16), x_lo.astype(BF16)

            def up(w):
                return (jnp.dot(x_hi, w[0:half, :], preferred_element_type=F32)
                        + jnp.dot(x_lo, w[half:2 * half, :], preferred_element_type=F32))

            h1 = up(w1b)
            hb = h1 * jax.nn.sigmoid(h1) * up(w3b)
            y_ref[rows, :] = _pack_halves(jnp.dot(hb.astype(BF16), w2b[...], preferred_element_type=F32))


def _expert_ffn(blk_e, n_used, rows, w1, w3, w2, layer):
    r, half = rows.shape
    d, de = w1.shape[2], w1.shape[3]
    bm = EXPERT_BLOCK
    nb = r // bm
    clamp = lambda b, nu: jnp.minimum(b, nu[0] - 1)
    row_spec = pl.BlockSpec((bm, half), lambda b, be, nu: (clamp(b, nu), 0))
    w_idx = lambda b, be, nu: (layer, be[clamp(b, nu)], 0, 0)
    return pl.pallas_call(
        _ffn_kernel,
        grid_spec=pltpu.PrefetchScalarGridSpec(
            num_scalar_prefetch=2,
            grid=(nb,),
            in_specs=[
                row_spec,
                pl.BlockSpec((1, 1, d, de), w_idx),
                pl.BlockSpec((1, 1, d, de), w_idx),
                pl.BlockSpec((1, 1, de, d), w_idx),
            ],
            out_specs=row_spec,
            scratch_shapes=[pltpu.VMEM((d, de), BF16), pltpu.VMEM((d, de), BF16), pltpu.VMEM((de, d), BF16)],
        ),
        out_shape=jax.ShapeDtypeStruct((r, half), jnp.uint32),
        compiler_params=pltpu.CompilerParams(dimension_semantics=("arbitrary",)),
        name="moe_expert_ffn",
    )(blk_e, n_used, rows, w1, w3, w2)


def _combine_ln_kernel(d0_ref, d1_ref, n0_ref, n1_ref, gate_ref, x_ref, y_hbm, g_ref, b_ref, out_ref,
                       ybuf, sems, *, alpha):
    t = COMBINE_TILE
    i = pl.program_id(0)

    def request_row(slot, d_refs, row, g, u):
        for k, d_ref in enumerate(d_refs):
            pltpu.make_async_copy(_hbm_row(y_hbm, d_ref[row]), _staged_row(ybuf.at[slot, k], g, u),
                                  sems.at[slot]).start(priority=k)

    def wait_tile(slot):
        def wait(g, u, r):
            for k in range(TOP_K):
                pltpu.make_async_copy(_hbm_row(y_hbm, 0), _staged_row(ybuf.at[slot, k], 0, 0), sems.at[slot]).wait()
        _for_row_groups(t, wait)

    def request_tile(slot, d_refs, first_row):
        _for_row_groups(t, lambda g, u, r: request_row(slot, d_refs, first_row + r, g, u))

    def reduce_tile(slot, first_row):
        wait_tile(slot)
        rows = pl.ds(first_row, t)
        gate = gate_ref[rows, :]
        hi0, lo0 = _unpack_halves(ybuf[slot, 0].reshape(t, -1))
        hi1, lo1 = _unpack_halves(ybuf[slot, 1].reshape(t, -1))
        g0, g1 = gate[:, 0:1], gate[:, 1:2]
        f = jnp.concatenate([g0 * hi0 + g1 * hi1, g0 * lo0 + g1 * lo1], axis=1)
        out_ref[rows, :] = _layer_norm(alpha * x_ref[rows, :] + f, g_ref[...], b_ref[...])

    @pl.when(i == 0)
    def _():
        request_tile(0, (d0_ref, d1_ref), 0)

    request_tile(1, (d0_ref, d1_ref), t)
    reduce_tile(0, 0)

    @pl.when(i + 1 < pl.num_programs(0))
    def _():
        request_tile(0, (n0_ref, n1_ref), 0)

    reduce_tile(1, t)


def _combine_ln(dest, gates_tok, x2d, y_rows, g, b, alpha):
    n, d = x2d.shape
    t = COMBINE_TILE
    n_tiles = n // t
    vec = pl.BlockSpec((1, d), lambda i: (0, 0))
    pair_spec = pl.BlockSpec((2 * t,), lambda i: (i,), memory_space=pltpu.SMEM)
    next_spec = pl.BlockSpec((t,), lambda i: (jnp.minimum(2 * i + 2, n_tiles - 1),), memory_space=pltpu.SMEM)
    return pl.pallas_call(
        functools.partial(_combine_ln_kernel, alpha=alpha),
        grid=(n_tiles // 2,),
        in_specs=[
            pair_spec, pair_spec, next_spec, next_spec,
            pl.BlockSpec((2 * t, TOP_K), lambda i: (i, 0)),
            pl.BlockSpec((2 * t, d), lambda i: (i, 0)),
            pl.BlockSpec(memory_space=pl.ANY),
            vec, vec,
        ],
        out_specs=pl.BlockSpec((2 * t, d), lambda i: (i, 0)),
        out_shape=jax.ShapeDtypeStruct((n, d), F32),
        scratch_shapes=[pltpu.VMEM((2, TOP_K, t // ROWS_PER_TRIP, ROWS_PER_TRIP, d // 2), jnp.uint32),
                        pltpu.SemaphoreType.DMA((2,))],
        compiler_params=pltpu.CompilerParams(dimension_semantics=("arbitrary",)),
        name="moe_combine_ln",
    )(dest[0], dest[1], dest[0], dest[1], gates_tok, x2d, y_rows, g, b)


def _rope_tables(positions):
    inv_freq = ROPE_THETA ** (-jnp.arange(0, QK_ROPE, 2, dtype=F32) / QK_ROPE)
    ang = positions.astype(F32)[..., None] * inv_freq
    cos, sin = jnp.cos(ang), jnp.sin(ang)
    b, s, _ = cos.shape
    zeros = jnp.zeros((b * s, QK_NOPE), F32)
    tail = jnp.zeros((b * s, LANES - QK_NOPE - QK_ROPE), F32)
    c2, s2 = cos.reshape(b * s, -1), sin.reshape(b * s, -1)
    return {
        'cos_t': jnp.swapaxes(cos, 1, 2), 'sin_t': jnp.swapaxes(sin, 1, 2),
        'cos_pad': jnp.concatenate([zeros, c2, c2, tail], axis=1),
        'sin_pad': jnp.concatenate([zeros, s2, s2, tail], axis=1),
    }


def _mla_weights(w_in, q_norm, w_uq, kv_norm, w_ukv, w_o):
    d = w_in.shape[0]
    lat_q_kv = Q_LORA + KV_LORA
    r1 = w_in[:, lat_q_kv:lat_q_kv + HALF_ROPE]
    r2 = w_in[:, lat_q_kv + HALF_ROPE:lat_q_kv + QK_ROPE]
    z_lo = jnp.zeros((d, QK_NOPE), F32)
    z_hi = jnp.zeros((d, LANES - QK_NOPE - QK_ROPE), F32)
    w_tok = jnp.concatenate([w_in[:, Q_LORA:lat_q_kv],
                             z_lo, r1, r2, z_hi,
                             z_lo, -r2, r1, z_hi],
                            axis=1)
    uq = w_uq.reshape(Q_LORA, N_HEADS, QK_NOPE + QK_ROPE)
    w_uq_t = jnp.concatenate([uq[:, :, :QK_NOPE].reshape(Q_LORA, -1),
                              uq[:, :, QK_NOPE:QK_NOPE + HALF_ROPE].reshape(Q_LORA, -1),
                              uq[:, :, QK_NOPE + HALF_ROPE:].reshape(Q_LORA, -1)], axis=1).T
    ukv = w_ukv.reshape(KV_LORA, N_HEADS, QK_NOPE + V_DIM)
    w_kn = jnp.concatenate([ukv[:, :, :QK_NOPE], jnp.zeros((KV_LORA, N_HEADS, HEAD_PAD - QK_NOPE), F32)],
                           axis=2).reshape(KV_LORA, N_HEADS * HEAD_PAD)
    w_v_t = ukv[:, :, QK_NOPE:].reshape(KV_LORA, N_HEADS * V_DIM).T
    return {
        'w_fm': w_in[:, :lat_q_kv].T.astype(BF16), 'w_tok': w_tok.astype(BF16),
        'w_uq_t': w_uq_t.astype(BF16), 'w_v_t': w_v_t.astype(BF16), 'w_kn': w_kn.astype(BF16),
        'qn_col': q_norm.reshape(-1, 1), 'kvn_col': kv_norm.reshape(-1, 1), 'kvn_row': kv_norm.reshape(1, -1),
        'w_o': w_o.astype(BF16),
    }


def _router_weights(w_grp, b_grp, w_exp, b_exp):
    d = w_grp.shape[0]
    pad = LANES - N_GROUPS - N_EXPERTS
    w = jnp.concatenate([w_grp, w_exp, jnp.zeros((d, pad), F32)], axis=1).T
    w_hi = w.astype(BF16)
    w_lo = (w - w_hi.astype(F32)).astype(BF16)
    bias = jnp.concatenate([b_grp, b_exp, jnp.zeros((pad,), F32)]).reshape(-1, 1)
    return w_hi, w_lo, bias


def _moe_ln(x2d, w_hi, w_lo, bias, tri, w1, w3, w2, layer, g, b, alpha, rows_buf):
    n, d = x2d.shape
    bm = EXPERT_BLOCK
    n_assign = n * TOP_K
    nb = -(-(n_assign + N_EXPERTS * (bm - 1)) // bm)
    if rows_buf is None:
        rows_buf = jnp.zeros((nb * bm, d // 2), jnp.uint32)
    eid, gates, rank, cnt = _router(x2d, w_hi, w_lo, bias, tri)
    counts = cnt[:, 0]
    padded = (counts + bm - 1) // bm * bm
    pend = jnp.cumsum(padded)
    pstart = (pend - padded).astype(I32)
    n_used = (pend[-1:] // bm).astype(I32)
    blk_start = jnp.arange(nb, dtype=I32) * bm
    blk_e = jnp.minimum(jnp.sum((pend[None, :] <= blk_start[:, None]).astype(I32), axis=1), N_EXPERTS - 1)
    dest = _slots(pstart, eid, rank)
    rows = _dispatch(dest, x2d, rows_buf)
    y_rows = _expert_ffn(blk_e, n_used, rows, w1, w3, w2, layer)
    return _combine_ln(dest, gates.T, x2d, y_rows, g, b, alpha), rows


def kernel(x, positions, ln_g, ln_b, mla_w_in, mla_q_norm, mla_w_uq, mla_kv_norm, mla_w_ukv, mla_w_o,
           pool_w, pool_b, pool_scale, moe_w_grp, moe_b_grp, moe_w_exp, moe_b_exp, moe_w1, moe_w3, moe_w2):
    batch, seq, d = x.shape
    depth = ln_g.shape[0]
    alpha = (2 * depth) ** 0.25
    tabs = _rope_tables(positions)
    t = ROUTE_TILE
    tri = (lax.broadcasted_iota(I32, (t, t), 0) < lax.broadcasted_iota(I32, (t, t), 1)).astype(BF16)
    x2d = x.reshape(batch * seq, d)
    vec = lambda v: v.reshape(1, d)
    rows_buf = None
    for i in range(depth):
        j = i // 2
        if i % 2 == 0:
            w = _mla_weights(mla_w_in[j], mla_q_norm[j], mla_w_uq[j], mla_kv_norm[j], mla_w_ukv[j], mla_w_o[j])
            q_t, k, v_t = _mla_proj(x2d, tabs, w, batch, seq)
            o_t = _attention(q_t, k, v_t)
            x2d = _outproj_ln(o_t, x2d, w['w_o'], vec(ln_g[i, 0]), vec(ln_b[i, 0]), alpha)
        else:
            x2d = _pool_ln(x2d, seq, pool_w[j].astype(BF16), vec(pool_b[j]), vec(pool_scale[j]),
                           vec(ln_g[i, 0]), vec(ln_b[i, 0]), alpha)
        w_hi, w_lo, bias = _router_weights(moe_w_grp[i], moe_b_grp[i], moe_w_exp[i], moe_b_exp[i])
        x2d, rows_buf = _moe_ln(x2d, w_hi, w_lo, bias, tri, moe_w1, moe_w3, moe_w2, i,
                                vec(ln_g[i, 1]), vec(ln_b[i, 1]), alpha, rows_buf)
    return x2d.reshape(batch, seq, d)
```

```python
import functools
import math

import jax
import jax.numpy as jnp
from jax import lax
from jax.experimental import pallas as pl
from jax.experimental.pallas import tpu as pltpu

F32 = jnp.float32
BF16 = jnp.bfloat16
I32 = jnp.int32

N_HEADS = 16
QK_NOPE = 64
QK_ROPE = 32
HALF_ROPE = QK_ROPE // 2
V_DIM = 64
Q_LORA = 256
KV_LORA = 128
ROPE_THETA = 10000.0
POOL_WINDOWS = (2, 4, 8, 16)
N_GROUPS = 8
EXP_PER_GROUP = 8
N_EXPERTS = N_GROUPS * EXP_PER_GROUP
TOP_K = 2
LN_EPS = 1e-5
RMS_EPS = 1e-6

LANES = 128
HEAD_PAD = 128
V_ROWS = 80
POOL_HALO = 16

TOK_TILE = 512
LN_TILE = 1024
ATT_TQ = 2048
ATT_TK = 1024
ATT_HEADS_PER_STEP = 1
ROUTE_TILE = 512
DISPATCH_TILE = 1024
COMBINE_TILE = 256
EXPERT_BLOCK = 512
FFN_CHUNK = 256

NEG_BIG = -1e30
_NT = (((1,), (1,)), ((), ()))
_TN = (((0,), (0,)), ((), ()))


def _layer_norm(y, g, b):
    mu = jnp.mean(y, axis=-1, keepdims=True)
    yc = y - mu
    var = jnp.mean(yc * yc, axis=-1, keepdims=True)
    return yc * lax.rsqrt(var + LN_EPS) * g + b


def _mla_proj_kernel(x_ref, cost_ref, sint_ref, cosp_ref, sinp_ref, wfm_ref, wtok_ref, wuq_ref,
                     wv_ref, wkn_ref, qn_ref, kvnc_ref, kvnr_ref, qt_ref, k_ref, vt_ref, *, q_scale):
    t = x_ref.shape[0]
    xb = x_ref[...].astype(BF16)
    lat_t = lax.dot_general(wfm_ref[...], xb, _NT, preferred_element_type=F32)
    cq = lat_t[0:Q_LORA]
    cq = cq * lax.rsqrt(jnp.mean(cq * cq, axis=0, keepdims=True) + RMS_EPS) * qn_ref[...]
    ckv = lat_t[Q_LORA:Q_LORA + KV_LORA]
    ckv = ckv * lax.rsqrt(jnp.mean(ckv * ckv, axis=0, keepdims=True) + RMS_EPS) * kvnc_ref[...]

    q_t = jnp.dot(wuq_ref[...], cq.astype(BF16), preferred_element_type=F32) * q_scale
    n_nope = N_HEADS * QK_NOPE
    n_half = N_HEADS * HALF_ROPE
    qn = q_t[0:n_nope].reshape(N_HEADS, QK_NOPE, t)
    x1 = q_t[n_nope:n_nope + n_half].reshape(N_HEADS, HALF_ROPE, t)
    x2 = q_t[n_nope + n_half:n_nope + 2 * n_half].reshape(N_HEADS, HALF_ROPE, t)
    c = cost_ref[...]
    s = sint_ref[...]
    qt_ref[0, :, 0:QK_NOPE, :] = qn.astype(BF16)
    qt_ref[0, :, QK_NOPE:QK_NOPE + HALF_ROPE, :] = (x1 * c - x2 * s).astype(BF16)
    qt_ref[0, :, QK_NOPE + HALF_ROPE:QK_NOPE + QK_ROPE, :] = (x2 * c + x1 * s).astype(BF16)
    qt_ref[0, :, QK_NOPE + QK_ROPE:HEAD_PAD, :] = jnp.zeros(
        (N_HEADS, HEAD_PAD - QK_NOPE - QK_ROPE, t), BF16)

    v_t = jnp.dot(wv_ref[...], ckv.astype(BF16), preferred_element_type=F32)
    vt_ref[0, :, 0:V_DIM, :] = v_t.reshape(N_HEADS, V_DIM, t).astype(BF16)
    vt_ref[0, :, V_DIM:V_ROWS, :] = jnp.ones((N_HEADS, V_ROWS - V_DIM, t), BF16)

    lat = jnp.dot(xb, wtok_ref[...], preferred_element_type=F32)
    ckv_tok = lat[:, 0:KV_LORA]
    ckv_tok = ckv_tok * lax.rsqrt(jnp.mean(ckv_tok * ckv_tok, axis=-1, keepdims=True) + RMS_EPS) * kvnr_ref[...]
    k_rope = lat[:, KV_LORA:KV_LORA + LANES] * cosp_ref[...] + lat[:, KV_LORA + LANES:] * sinp_ref[...]
    kn = jnp.dot(ckv_tok.astype(BF16), wkn_ref[...], preferred_element_type=F32)
    for h in range(N_HEADS):
        k_ref[0, h] = (kn[:, h * HEAD_PAD:(h + 1) * HEAD_PAD] + k_rope).astype(BF16)


def _mla_proj(x2d, tabs, w, batch, seq):
    n, d = x2d.shape
    t = TOK_TILE
    tiles_per_seq = seq // t
    q_scale = (QK_NOPE + QK_ROPE) ** -0.5 * math.log2(math.e)
    full = lambda a: pl.BlockSpec(a.shape, lambda i: (0,) * a.ndim)
    bs_idx = lambda i: (i // tiles_per_seq, 0, 0, i % tiles_per_seq)
    return pl.pallas_call(
        functools.partial(_mla_proj_kernel, q_scale=q_scale),
        grid=(n // t,),
        in_specs=[
            pl.BlockSpec((t, d), lambda i: (i, 0)),
            pl.BlockSpec((1, HALF_ROPE, t), lambda i: (i // tiles_per_seq, 0, i % tiles_per_seq)),
            pl.BlockSpec((1, HALF_ROPE, t), lambda i: (i // tiles_per_seq, 0, i % tiles_per_seq)),
            pl.BlockSpec((t, LANES), lambda i: (i, 0)),
            pl.BlockSpec((t, LANES), lambda i: (i, 0)),
            full(w['w_fm']), full(w['w_tok']), full(w['w_uq_t']), full(w['w_v_t']), full(w['w_kn']),
            full(w['qn_col']), full(w['kvn_col']), full(w['kvn_row']),
        ],
        out_specs=[
            pl.BlockSpec((1, N_HEADS, HEAD_PAD, t), bs_idx),
            pl.BlockSpec((1, N_HEADS, t, HEAD_PAD), lambda i: (i // tiles_per_seq, 0, i % tiles_per_seq, 0)),
            pl.BlockSpec((1, N_HEADS, V_ROWS, t), bs_idx),
        ],
        out_shape=[
            jax.ShapeDtypeStruct((batch, N_HEADS, HEAD_PAD, seq), BF16),
            jax.ShapeDtypeStruct((batch, N_HEADS, seq, HEAD_PAD), BF16),
            jax.ShapeDtypeStruct((batch, N_HEADS, V_ROWS, seq), BF16),
        ],
        compiler_params=pltpu.CompilerParams(dimension_semantics=("arbitrary",),
                                             vmem_limit_bytes=56 * 1024 * 1024),
        name="mla_proj",
    )(x2d, tabs['cos_t'], tabs['sin_t'], tabs['cos_pad'], tabs['sin_pad'],
      w['w_fm'], w['w_tok'], w['w_uq_t'], w['w_v_t'], w['w_kn'], w['qn_col'], w['kvn_col'], w['kvn_row'])


def _attn_kernel(qt_ref, k_ref, vt_ref, o_ref, s_scr, p_scr):
    n_h = qt_ref.shape[1]
    seq = k_ref.shape[2]
    tq, tk = ATT_TQ, ATT_TK
    n_q = seq // tq
    heads = range(n_h)

    def load_q(q0):
        return [qt_ref[0, h, :, pl.ds(q0, tq)] for h in heads]

    def scores(h, q_t, k0, slot):
        kb = k_ref[0, h, pl.ds(k0, tk), :]
        s = jnp.dot(kb, q_t, preferred_element_type=F32)
        s_scr[h, slot] = s
        return jnp.max(s, axis=0, keepdims=True)

    def softmax(h, slot, m, bmax):
        s = s_scr[h, slot]
        m_new = jnp.maximum(m, bmax)
        p_scr[h, slot] = jnp.exp2(s - m_new).astype(BF16)
        return m_new, jnp.exp2(m - m_new)

    def values(h, k0, slot, alpha, acc):
        vb = vt_ref[0, h, :, pl.ds(k0, tk)]
        return alpha * acc + jnp.dot(vb, p_scr[h, slot], preferred_element_type=F32)

    def trip(t, carry, q_ts):
        m, acc, a_pend, bmax0 = [list(c) for c in carry]
        ka = pl.multiple_of(t * tq, tq)
        k_prev = pl.multiple_of(jnp.maximum(ka - tk, 0), tk)
        a0, bmax1 = [None] * n_h, [None] * n_h
        for h in heads:
            bmax1[h] = scores(h, q_ts[h], ka + tk, 1)
        for h in heads:
            m[h], a0[h] = softmax(h, 0, m[h], bmax0[h])
        for h in heads:
            acc[h] = values(h, k_prev, 1, a_pend[h], acc[h])
        for h in heads:
            bmax0[h] = scores(h, q_ts[h], ka + tq, 0)
        for h in heads:
            m[h], a_pend[h] = softmax(h, 1, m[h], bmax1[h])
        for h in heads:
            acc[h] = values(h, ka, 0, a0[h], acc[h])
        return tuple(m), tuple(acc), tuple(a_pend), tuple(bmax0)

    def diagonal_trip(q0, carry, q_ts, q_next):
        m, acc, a_pend, _ = carry
        k_prev = pl.multiple_of(jnp.maximum(q0 - tk, 0), tk)
        tri = lax.broadcasted_iota(I32, (tk, tk), 0) <= lax.broadcasted_iota(I32, (tk, tk), 1)
        bmax_next = []
        for h in heads:
            kb = k_ref[0, h, pl.ds(q0 + tk, tk), :]
            s_b = jnp.dot(kb, q_ts[h][:, tk:], preferred_element_type=F32)
            s_a = s_scr[h, 0]
            s_al = jnp.where(tri, s_a[:, :tk], NEG_BIG)
            s_ar = s_a[:, tk:]
            bmax_a = jnp.concatenate([jnp.max(s_al, axis=0, keepdims=True),
                                      jnp.max(s_ar, axis=0, keepdims=True)], axis=1)
            m_a = jnp.maximum(m[h], bmax_a)
            a0 = jnp.exp2(m[h] - m_a)
            p_scr[h, 0] = jnp.concatenate([jnp.exp2(s_al - m_a[:, :tk]), jnp.exp2(s_ar - m_a[:, tk:])],
                                          axis=1).astype(BF16)
            acc_h = values(h, k_prev, 1, a_pend[h], acc[h])
            bmax_next.append(scores(h, q_next[h], 0, 0))
            s_b = jnp.where(tri, s_b, NEG_BIG)
            m_r = jnp.maximum(m_a[:, tk:], jnp.max(s_b, axis=0, keepdims=True))
            p_b = jnp.exp2(s_b - m_r).astype(BF16)
            acc_h = values(h, q0, 0, a0, acc_h)
            vb = vt_ref[0, h, :, pl.ds(q0 + tk, tk)]
            acc_r = jnp.exp2(m_a[:, tk:] - m_r) * acc_h[:, tk:] + jnp.dot(vb, p_b, preferred_element_type=F32)
            out = jnp.concatenate([acc_h[:, :tk], acc_r], axis=1)
            o_ref[0, h * V_DIM:(h + 1) * V_DIM, pl.ds(q0, tq)] = (
                out[0:V_DIM] / out[V_DIM:V_DIM + 1]).astype(BF16)
        return tuple(bmax_next)

    def q_tile(qi, bmax0):
        q0 = pl.multiple_of(qi * tq, tq)
        q_ts = load_q(q0)
        q_next = load_q(pl.multiple_of(jnp.minimum(q0 + tq, seq - tq), tq))
        for h in heads:
            p_scr[h, 1] = jnp.zeros((tk, tq), BF16)
        init = (tuple(jnp.full((1, tq), NEG_BIG, F32) for _ in heads),
                tuple(jnp.zeros((V_ROWS, tq), F32) for _ in heads),
                tuple(jnp.ones((1, tq), F32) for _ in heads),
                bmax0)
        res = lax.fori_loop(0, qi, lambda t, c: trip(t, c, q_ts), init)
        return diagonal_trip(q0, res, q_ts, q_next)

    q_first = load_q(0)
    lax.fori_loop(0, n_q, q_tile, tuple(scores(h, q_first[h], 0, 0) for h in heads))


def _attention(q_t, k, v_t):
    batch, heads, _, seq = q_t.shape
    g = ATT_HEADS_PER_STEP
    assert ATT_TQ == 2 * ATT_TK and seq % ATT_TQ == 0 and heads % g == 0
    return pl.pallas_call(
        _attn_kernel,
        grid=(batch, heads // g),
        in_specs=[
            pl.BlockSpec((1, g, HEAD_PAD, seq), lambda b, h: (b, h, 0, 0)),
            pl.BlockSpec((1, g, seq, HEAD_PAD), lambda b, h: (b, h, 0, 0)),
            pl.BlockSpec((1, g, V_ROWS, seq), lambda b, h: (b, h, 0, 0)),
        ],
        out_specs=pl.BlockSpec((1, g * V_DIM, seq), lambda b, h: (b, h, 0)),
        out_shape=jax.ShapeDtypeStruct((batch, heads * V_DIM, seq), BF16),
        scratch_shapes=[pltpu.VMEM((g, 2, ATT_TK, ATT_TQ), F32), pltpu.VMEM((g, 2, ATT_TK, ATT_TQ), BF16)],
        compiler_params=pltpu.CompilerParams(dimension_semantics=("arbitrary", "arbitrary"),
                                             vmem_limit_bytes=56 * 1024 * 1024),
        name="mla_attention",
    )(q_t, k, v_t)


def _outproj_ln_kernel(ot_ref, x_ref, wo_ref, g_ref, b_ref, out_ref, *, alpha):
    f = lax.dot_general(ot_ref[0], wo_ref[...], _TN, preferred_element_type=F32)
    out_ref[...] = _layer_norm(alpha * x_ref[...] + f, g_ref[...], b_ref[...])


def _outproj_ln(o_t, x2d, w_o, g, b, alpha):
    n, d = x2d.shape
    batch, hd, seq = o_t.shape
    t = LN_TILE
    tiles_per_seq = seq // t
    return pl.pallas_call(
        functools.partial(_outproj_ln_kernel, alpha=alpha),
        grid=(n // t,),
        in_specs=[
            pl.BlockSpec((1, hd, t), lambda i: (i // tiles_per_seq, 0, i % tiles_per_seq)),
            pl.BlockSpec((t, d), lambda i: (i, 0)),
            pl.BlockSpec((hd, d), lambda i: (0, 0)),
            pl.BlockSpec((1, d), lambda i: (0, 0)),
            pl.BlockSpec((1, d), lambda i: (0, 0)),
        ],
        out_specs=pl.BlockSpec((t, d), lambda i: (i, 0)),
        out_shape=jax.ShapeDtypeStruct((n, d), F32),
        compiler_params=pltpu.CompilerParams(dimension_semantics=("arbitrary",)),
        name="mla_outproj_ln",
    )(o_t, x2d, w_o, g, b)


def _pool_ln_kernel(x_ref, halo_ref, w_ref, pb_ref, ps_ref, g_ref, b_ref, out_ref, *, alpha, tiles_per_seq):
    t, d = x_ref.shape
    gd = d // len(POOL_WINDOWS)
    i = pl.program_id(0)
    tile_in_seq = i % tiles_per_seq
    x = x_ref[...]
    halo = jnp.where(tile_in_seq == 0, 0.0, halo_ref[...])
    ext = jnp.concatenate([halo, x], axis=0)
    pos = tile_in_seq * t + lax.broadcasted_iota(I32, (t, 1), 0)
    ys = []
    for gi, win in enumerate(POOL_WINDOWS):
        e = ext[:, gi * gd:(gi + 1) * gd]
        sh = 1
        while sh < win:
            e = e + pltpu.roll(e, sh, axis=0)
            sh *= 2
        inv_cnt = 1.0 / jnp.minimum(pos + 1, win).astype(F32)
        pooled = e[POOL_HALO:] * inv_cnt - x[:, gi * gd:(gi + 1) * gd]
        ys.append(jnp.dot(pooled.astype(BF16), w_ref[gi], preferred_element_type=F32))
    y = (jnp.concatenate(ys, axis=1) + pb_ref[...]) * ps_ref[...]
    out_ref[...] = _layer_norm(alpha * x + y, g_ref[...], b_ref[...])


def _pool_ln(x2d, seq, w, pb, ps, g, b, alpha):
    n, d = x2d.shape
    t = LN_TILE
    tiles_per_seq = seq // t
    halo_blocks = t // POOL_HALO
    vec = pl.BlockSpec((1, d), lambda i: (0, 0))
    return pl.pallas_call(
        functools.partial(_pool_ln_kernel, alpha=alpha, tiles_per_seq=tiles_per_seq),
        grid=(n // t,),
        in_specs=[
            pl.BlockSpec((t, d), lambda i: (i, 0)),
            pl.BlockSpec((POOL_HALO, d), lambda i: (jnp.maximum(i * halo_blocks - 1, 0), 0)),
            pl.BlockSpec(w.shape, lambda i: (0, 0, 0)),
            vec, vec, vec, vec,
        ],
        out_specs=pl.BlockSpec((t, d), lambda i: (i, 0)),
        out_shape=jax.ShapeDtypeStruct((n, d), F32),
        compiler_params=pltpu.CompilerParams(dimension_semantics=("arbitrary",)),
        name="pool_ln",
    )(x2d, x2d, w, pb, ps, g, b)


def _router_kernel(x_ref, whi_ref, wlo_ref, bias_ref, tri_ref, eid_ref, gate_ref, rank_ref, cnt_ref, run_ref):
    t = x_ref.shape[0]

    @pl.when(pl.program_id(0) == 0)
    def _():
        run_ref[...] = jnp.zeros_like(run_ref)

    x = x_ref[...]
    xh = x.astype(BF16)
    xl = (x - xh.astype(F32)).astype(BF16)
    lg = (lax.dot_general(whi_ref[...], xh, _NT, preferred_element_type=F32)
          + lax.dot_general(whi_ref[...], xl, _NT, preferred_element_type=F32)
          + lax.dot_general(wlo_ref[...], xh, _NT, preferred_element_type=F32)
          + bias_ref[...])
    ridx = lax.broadcasted_iota(I32, (N_GROUPS, t), 0)

    g = lg[0:N_GROUPS]
    gmax = jnp.max(g, axis=0, keepdims=True)
    g_sel = jnp.min(jnp.where(g == gmax, ridx, N_GROUPS), axis=0, keepdims=True)
    g_gate = 1.0 / jnp.sum(jnp.exp(g - gmax), axis=0, keepdims=True)

    e_all = lg[N_GROUPS:N_GROUPS + N_EXPERTS]
    e = jnp.zeros((EXP_PER_GROUP, t), F32)
    for gi in range(N_GROUPS):
        e = e + jnp.where(g_sel == gi, e_all[gi * EXP_PER_GROUP:(gi + 1) * EXP_PER_GROUP], 0.0)
    pe = jnp.exp(e - jnp.max(e, axis=0, keepdims=True))
    p1 = jnp.max(pe, axis=0, keepdims=True)
    i1 = jnp.min(jnp.where(pe == p1, ridx, EXP_PER_GROUP), axis=0, keepdims=True)
    pe2 = jnp.where(ridx == i1, -1.0, pe)
    p2 = jnp.max(pe2, axis=0, keepdims=True)
    i2 = jnp.min(jnp.where(pe2 == p2, ridx, EXP_PER_GROUP), axis=0, keepdims=True)
    denom = p1 + p2
    e1 = g_sel * EXP_PER_GROUP + i1
    e2 = g_sel * EXP_PER_GROUP + i2
    eid_ref[0:1, :] = e1
    eid_ref[1:2, :] = e2
    gate_ref[0:1, :] = g_gate * p1 / denom
    gate_ref[1:2, :] = g_gate * p2 / denom

    eidx = lax.broadcasted_iota(I32, (N_EXPERTS, t), 0)
    oh1 = eidx == e1
    oh2 = eidx == e2
    oh = jnp.where(oh1 | oh2, 1.0, 0.0)
    earlier = jnp.dot(oh.astype(BF16), tri_ref[...], preferred_element_type=F32)
    tot = earlier + run_ref[:, 0:1]
    rank_ref[0:1, :] = jnp.sum(jnp.where(oh1, tot, 0.0), axis=0, keepdims=True).astype(I32)
    rank_ref[1:2, :] = jnp.sum(jnp.where(oh2, tot, 0.0), axis=0, keepdims=True).astype(I32)
    run_ref[...] = run_ref[...] + jnp.sum(oh, axis=1, keepdims=True)
    cnt_ref[...] = run_ref[...].astype(I32)


def _router(x2d, w_hi, w_lo, bias, tri):
    n, d = x2d.shape
    t = ROUTE_TILE
    tok = lambda dt: jax.ShapeDtypeStruct((TOP_K, n), dt)
    tok_spec = pl.BlockSpec((TOP_K, t), lambda i: (0, i))
    return pl.pallas_call(
        _router_kernel,
        grid=(n // t,),
        in_specs=[
            pl.BlockSpec((t, d), lambda i: (i, 0)),
            pl.BlockSpec(w_hi.shape, lambda i: (0, 0)),
            pl.BlockSpec(w_lo.shape, lambda i: (0, 0)),
            pl.BlockSpec(bias.shape, lambda i: (0, 0)),
            pl.BlockSpec(tri.shape, lambda i: (0, 0)),
        ],
        out_specs=[tok_spec, tok_spec, tok_spec, pl.BlockSpec((N_EXPERTS, LANES), lambda i: (0, 0))],
        out_shape=[tok(I32), tok(F32), tok(I32), jax.ShapeDtypeStruct((N_EXPERTS, LANES), I32)],
        scratch_shapes=[pltpu.VMEM((N_EXPERTS, LANES), F32)],
        compiler_params=pltpu.CompilerParams(dimension_semantics=("arbitrary",)),
        name="moe_router",
    )(x2d, w_hi, w_lo, bias, tri)


ROWS_PER_TRIP = 8


def _pack_halves(y):
    half = y.shape[1] // 2
    bits = lax.bitcast_convert_type(y.astype(BF16).astype(F32), jnp.uint32)
    return bits[:, :half] | (bits[:, half:] >> 16)


def _unpack_halves(p):
    hi = lax.bitcast_convert_type(p & jnp.uint32(0xFFFF0000), F32)
    lo = lax.bitcast_convert_type(p << 16, F32)
    return hi, lo


def _for_row_groups(n_rows, body):
    def trip(g, c):
        base = g * ROWS_PER_TRIP
        for u in range(ROWS_PER_TRIP):
            body(g, u, base + u)
        return c
    lax.fori_loop(0, n_rows // ROWS_PER_TRIP, trip, 0)


def _staged_row(buf, g, u):
    return buf.at[g, pl.ds(u, 1)]


def _hbm_row(arr, row):
    return arr.at[pl.ds(row, 1)]


def _slots_kernel(pstart_ref, eid_ref, rank_ref, dest_ref):
    eid = eid_ref[...]

    def add_expert(e, acc):
        return acc + jnp.where(eid == e, pstart_ref[e], 0)

    dest_ref[...] = lax.fori_loop(0, N_EXPERTS, add_expert, rank_ref[...])


def _slots(pstart, eid, rank):
    full = pl.BlockSpec(eid.shape, lambda i, ps: (0, 0))
    return pl.pallas_call(
        _slots_kernel,
        grid_spec=pltpu.PrefetchScalarGridSpec(num_scalar_prefetch=1, grid=(1,),
                                               in_specs=[full, full], out_specs=full),
        out_shape=jax.ShapeDtypeStruct(eid.shape, I32),
        name="moe_slots",
    )(pstart, eid, rank)


def _dispatch_kernel(d0_ref, d1_ref, x_ref, rows_in_hbm, rows_hbm, xp_scr, sem):
    del rows_in_hbm
    t = x_ref.shape[0]
    xp_scr[...] = _pack_halves(x_ref[...]).reshape(xp_scr.shape)

    def issue(g, u, r):
        for k, d_ref in enumerate((d0_ref, d1_ref)):
            pltpu.make_async_copy(_staged_row(xp_scr, g, u), _hbm_row(rows_hbm, d_ref[r]), sem).start(priority=k)

    def drain(g, u, r):
        for _ in range(TOP_K):
            pltpu.make_async_copy(_staged_row(xp_scr, 0, 0), _hbm_row(rows_hbm, 0), sem).wait()

    _for_row_groups(t, issue)
    _for_row_groups(t, drain)


def _dispatch(dest, x2d, rows_init):
    n, d = x2d.shape
    t = DISPATCH_TILE
    tok_spec = pl.BlockSpec((t,), lambda i: (i,), memory_space=pltpu.SMEM)
    return pl.pallas_call(
        _dispatch_kernel,
        grid=(n // t,),
        in_specs=[
            tok_spec, tok_spec,
            pl.BlockSpec((t, d), lambda i: (i, 0)),
            pl.BlockSpec(memory_space=pl.ANY),
        ],
        out_specs=pl.BlockSpec(memory_space=pl.ANY),
        out_shape=jax.ShapeDtypeStruct(rows_init.shape, rows_init.dtype),
        scratch_shapes=[pltpu.VMEM((t // ROWS_PER_TRIP, ROWS_PER_TRIP, d // 2), jnp.uint32),
                        pltpu.SemaphoreType.DMA(())],
        input_output_aliases={3: 0},
        compiler_params=pltpu.CompilerParams(dimension_semantics=("arbitrary",)),
        name="moe_dispatch",
    )(dest[0], dest[1], x2d, rows_init)


def _ffn_kernel(be_ref, nu_ref, rows_ref, w1_ref, w3_ref, w2_ref, y_ref, w1b, w3b, w2b):
    b = pl.program_id(0)
    half = rows_ref.shape[1]

    @pl.when(b < nu_ref[0])
    def _():
        prev = be_ref[jnp.maximum(b - 1, 0)]

        @pl.when((b == 0) | (be_ref[b] != prev))
        def _():
            w1b[...] = w1_ref[0, 0].astype(BF16)
            w3b[...] = w3_ref[0, 0].astype(BF16)
            w2b[...] = w2_ref[0, 0].astype(BF16)

        for c in range(rows_ref.shape[0] // FFN_CHUNK):
            rows = pl.ds(c * FFN_CHUNK, FFN_CHUNK)
            x_hi, x_lo = _unpack_halves(rows_ref[rows, :])
            x_hi, x_lo = x_hi.astype(BF16), x_lo.astype(BF16)

            def up(w):
                return (jnp.dot(x_hi, w[0:half, :], preferred_element_type=F32)
                        + jnp.dot(x_lo, w[half:2 * half, :], preferred_element_type=F32))

            h1 = up(w1b)
            hb = h1 * jax.nn.sigmoid(h1) * up(w3b)
            y_ref[rows, :] = _pack_halves(jnp.dot(hb.astype(BF16), w2b[...], preferred_element_type=F32))


def _expert_ffn(blk_e, n_used, rows, w1, w3, w2, layer):
    r, half = rows.shape
    d, de = w1.shape[2], w1.shape[3]
    bm = EXPERT_BLOCK
    nb = r // bm
    clamp = lambda b, nu: jnp.minimum(b, nu[0] - 1)
    row_spec = pl.BlockSpec((bm, half), lambda b, be, nu: (clamp(b, nu), 0))
    w_idx = lambda b, be, nu: (layer, be[clamp(b, nu)], 0, 0)
    return pl.pallas_call(
        _ffn_kernel,
        grid_spec=pltpu.PrefetchScalarGridSpec(
            num_scalar_prefetch=2,
            grid=(nb,),
            in_specs=[
                row_spec,
                pl.BlockSpec((1, 1, d, de), w_idx),
                pl.BlockSpec((1, 1, d, de), w_idx),
                pl.BlockSpec((1, 1, de, d), w_idx),
            ],
            out_specs=row_spec,
            scratch_shapes=[pltpu.VMEM((d, de), BF16), pltpu.VMEM((d, de), BF16), pltpu.VMEM((de, d), BF16)],
        ),
        out_shape=jax.ShapeDtypeStruct((r, half), jnp.uint32),
        compiler_params=pltpu.CompilerParams(dimension_semantics=("arbitrary",)),
        name="moe_expert_ffn",
    )(blk_e, n_used, rows, w1, w3, w2)


def _combine_ln_kernel(d0_ref, d1_ref, n0_ref, n1_ref, gate_ref, x_ref, y_hbm, g_ref, b_ref, out_ref,
                       ybuf, sems, *, alpha):
    t = COMBINE_TILE
    i = pl.program_id(0)

    def request_row(slot, d_refs, row, g, u):
        for k, d_ref in enumerate(d_refs):
            pltpu.make_async_copy(_hbm_row(y_hbm, d_ref[row]), _staged_row(ybuf.at[slot, k], g, u),
                                  sems.at[slot]).start(priority=k)

    def wait_tile(slot):
        def wait(g, u, r):
            for k in range(TOP_K):
                pltpu.make_async_copy(_hbm_row(y_hbm, 0), _staged_row(ybuf.at[slot, k], 0, 0), sems.at[slot]).wait()
        _for_row_groups(t, wait)

    def request_tile(slot, d_refs, first_row):
        _for_row_groups(t, lambda g, u, r: request_row(slot, d_refs, first_row + r, g, u))

    def reduce_tile(slot, first_row):
        wait_tile(slot)
        rows = pl.ds(first_row, t)
        gate = gate_ref[rows, :]
        hi0, lo0 = _unpack_halves(ybuf[slot, 0].reshape(t, -1))
        hi1, lo1 = _unpack_halves(ybuf[slot, 1].reshape(t, -1))
        g0, g1 = gate[:, 0:1], gate[:, 1:2]
        f = jnp.concatenate([g0 * hi0 + g1 * hi1, g0 * lo0 + g1 * lo1], axis=1)
        out_ref[rows, :] = _layer_norm(alpha * x_ref[rows, :] + f, g_ref[...], b_ref[...])

    @pl.when(i == 0)
    def _():
        request_tile(0, (d0_ref, d1_ref), 0)

    request_tile(1, (d0_ref, d1_ref), t)
    reduce_tile(0, 0)

    @pl.when(i + 1 < pl.num_programs(0))
    def _():
        request_tile(0, (n0_ref, n1_ref), 0)

    reduce_tile(1, t)


def _combine_ln(dest, gates_tok, x2d, y_rows, g, b, alpha):
    n, d = x2d.shape
    t = COMBINE_TILE
    n_tiles = n // t
    vec = pl.BlockSpec((1, d), lambda i: (0, 0))
    pair_spec = pl.BlockSpec((2 * t,), lambda i: (i,), memory_space=pltpu.SMEM)
    next_spec = pl.BlockSpec((t,), lambda i: (jnp.minimum(2 * i + 2, n_tiles - 1),), memory_space=pltpu.SMEM)
    return pl.pallas_call(
        functools.partial(_combine_ln_kernel, alpha=alpha),
        grid=(n_tiles // 2,),
        in_specs=[
            pair_spec, pair_spec, next_spec, next_spec,
            pl.BlockSpec((2 * t, TOP_K), lambda i: (i, 0)),
            pl.BlockSpec((2 * t, d), lambda i: (i, 0)),
            pl.BlockSpec(memory_space=pl.ANY),
            vec, vec,
        ],
        out_specs=pl.BlockSpec((2 * t, d), lambda i: (i, 0)),
        out_shape=jax.ShapeDtypeStruct((n, d), F32),
        scratch_shapes=[pltpu.VMEM((2, TOP_K, t // ROWS_PER_TRIP, ROWS_PER_TRIP, d // 2), jnp.uint32),
                        pltpu.SemaphoreType.DMA((2,))],
        compiler_params=pltpu.CompilerParams(dimension_semantics=("arbitrary",)),
        name="moe_combine_ln",
    )(dest[0], dest[1], dest[0], dest[1], gates_tok, x2d, y_rows, g, b)


def _rope_tables(positions):
    inv_freq = ROPE_THETA ** (-jnp.arange(0, QK_ROPE, 2, dtype=F32) / QK_ROPE)
    ang = positions.astype(F32)[..., None] * inv_freq
    cos, sin = jnp.cos(ang), jnp.sin(ang)
    b, s, _ = cos.shape
    zeros = jnp.zeros((b * s, QK_NOPE), F32)
    tail = jnp.zeros((b * s, LANES - QK_NOPE - QK_ROPE), F32)
    c2, s2 = cos.reshape(b * s, -1), sin.reshape(b * s, -1)
    return {
        'cos_t': jnp.swapaxes(cos, 1, 2), 'sin_t': jnp.swapaxes(sin, 1, 2),
        'cos_pad': jnp.concatenate([zeros, c2, c2, tail], axis=1),
        'sin_pad': jnp.concatenate([zeros, s2, s2, tail], axis=1),
    }


def _mla_weights(w_in, q_norm, w_uq, kv_norm, w_ukv, w_o):
    d = w_in.shape[0]
    lat_q_kv = Q_LORA + KV_LORA
    r1 = w_in[:, lat_q_kv:lat_q_kv + HALF_ROPE]
    r2 = w_in[:, lat_q_kv + HALF_ROPE:lat_q_kv + QK_ROPE]
    z_lo = jnp.zeros((d, QK_NOPE), F32)
    z_hi = jnp.zeros((d, LANES - QK_NOPE - QK_ROPE), F32)
    w_tok = jnp.concatenate([w_in[:, Q_LORA:lat_q_kv],
                             z_lo, r1, r2, z_hi,
                             z_lo, -r2, r1, z_hi],
                            axis=1)
    uq = w_uq.reshape(Q_LORA, N_HEADS, QK_NOPE + QK_ROPE)
    w_uq_t = jnp.concatenate([uq[:, :, :QK_NOPE].reshape(Q_LORA, -1),
                              uq[:, :, QK_NOPE:QK_NOPE + HALF_ROPE].reshape(Q_LORA, -1),
                              uq[:, :, QK_NOPE + HALF_ROPE:].reshape(Q_LORA, -1)], axis=1).T
    ukv = w_ukv.reshape(KV_LORA, N_HEADS, QK_NOPE + V_DIM)
    w_kn = jnp.concatenate([ukv[:, :, :QK_NOPE], jnp.zeros((KV_LORA, N_HEADS, HEAD_PAD - QK_NOPE), F32)],
                           axis=2).reshape(KV_LORA, N_HEADS * HEAD_PAD)
    w_v_t = ukv[:, :, QK_NOPE:].reshape(KV_LORA, N_HEADS * V_DIM).T
    return {
        'w_fm': w_in[:, :lat_q_kv].T.astype(BF16), 'w_tok': w_tok.astype(BF16),
        'w_uq_t': w_uq_t.astype(BF16), 'w_v_t': w_v_t.astype(BF16), 'w_kn': w_kn.astype(BF16),
        'qn_col': q_norm.reshape(-1, 1), 'kvn_col': kv_norm.reshape(-1, 1), 'kvn_row': kv_norm.reshape(1, -1),
        'w_o': w_o.astype(BF16),
    }


def _router_weights(w_grp, b_grp, w_exp, b_exp):
    d = w_grp.shape[0]
    pad = LANES - N_GROUPS - N_EXPERTS
    w = jnp.concatenate([w_grp, w_exp, jnp.zeros((d, pad), F32)], axis=1).T
    w_hi = w.astype(BF16)
    w_lo = (w - w_hi.astype(F32)).astype(BF16)
    bias = jnp.concatenate([b_grp, b_exp, jnp.zeros((pad,), F32)]).reshape(-1, 1)
    return w_hi, w_lo, bias


def _moe_ln(x2d, w_hi, w_lo, bias, tri, w1, w3, w2, layer, g, b, alpha, rows_buf):
    n, d = x2d.shape
    bm = EXPERT_BLOCK
    n_assign = n * TOP_K
    nb = -(-(n_assign + N_EXPERTS * (bm - 1)) // bm)
    if rows_buf is None:
        rows_buf = jnp.zeros((nb * bm, d // 2), jnp.uint32)
    eid, gates, rank, cnt = _router(x2d, w_hi, w_lo, bias, tri)
    counts = cnt[:, 0]
    padded = (counts + bm - 1) // bm * bm
    pend = jnp.cumsum(padded)
    pstart = (pend - padded).astype(I32)
    n_used = (pend[-1:] // bm).astype(I32)
    blk_start = jnp.arange(nb, dtype=I32) * bm
    blk_e = jnp.minimum(jnp.sum((pend[None, :] <= blk_start[:, None]).astype(I32), axis=1), N_EXPERTS - 1)
    dest = _slots(pstart, eid, rank)
    rows = _dispatch(dest, x2d, rows_buf)
    y_rows = _expert_ffn(blk_e, n_used, rows, w1, w3, w2, layer)
    return _combine_ln(dest, gates.T, x2d, y_rows, g, b, alpha), rows


def kernel(x, positions, ln_g, ln_b, mla_w_in, mla_q_norm, mla_w_uq, mla_kv_norm, mla_w_ukv, mla_w_o,
           pool_w, pool_b, pool_scale, moe_w_grp, moe_b_grp, moe_w_exp, moe_b_exp, moe_w1, moe_w3, moe_w2):
    batch, seq, d = x.shape
    depth = ln_g.shape[0]
    alpha = (2 * depth) ** 0.25
    tabs = _rope_tables(positions)
    t = ROUTE_TILE
    tri = (lax.broadcasted_iota(I32, (t, t), 0) < lax.broadcasted_iota(I32, (t, t), 1)).astype(BF16)
    x2d = x.reshape(batch * seq, d)
    vec = lambda v: v.reshape(1, d)
    rows_buf = None
    for i in range(depth):
        j = i // 2
        if i % 2 == 0:
            w = _mla_weights(mla_w_in[j], mla_q_norm[j], mla_w_uq[j], mla_kv_norm[j], mla_w_ukv[j], mla_w_o[j])
            q_t, k, v_t = _mla_proj(x2d, tabs, w, batch, seq)
            o_t = _attention(q_t, k, v_t)
            x2d = _outproj_ln(o_t, x2d, w['w_o'], vec(ln_g[i, 0]), vec(ln_b[i, 0]), alpha)
        else:
            x2d = _pool_ln(x2d, seq, pool_w[j].astype(BF16), vec(pool_b[j]), vec(pool_scale[j]),
                           vec(ln_g[i, 0]), vec(ln_b[i, 0]), alpha)
        w_hi, w_lo, bias = _router_weights(moe_w_grp[i], moe_b_grp[i], moe_w_exp[i], moe_b_exp[i])
        x2d, rows_buf = _moe_ln(x2d, w_hi, w_lo, bias, tri, moe_w1, moe_w3, moe_w2, i,
                                vec(ln_g[i, 1]), vec(ln_b[i, 1]), alpha, rows_buf)
    return x2d.reshape(batch, seq, d)
```

```python
import functools
import math

import jax
import jax.numpy as jnp
from jax import lax
from jax.experimental import pallas as pl
from jax.experimental.pallas import tpu as pltpu

F32 = jnp.float32
BF16 = jnp.bfloat16
I32 = jnp.int32

N_HEADS = 16
QK_NOPE = 64
QK_ROPE = 32
HALF_ROPE = QK_ROPE // 2
V_DIM = 64
Q_LORA = 256
KV_LORA = 128
ROPE_THETA = 10000.0
POOL_WINDOWS = (2, 4, 8, 16)
N_GROUPS = 8
EXP_PER_GROUP = 8
N_EXPERTS = N_GROUPS * EXP_PER_GROUP
TOP_K = 2
LN_EPS = 1e-5
RMS_EPS = 1e-6

LANES = 128
HEAD_PAD = 128
V_ROWS = 80
POOL_HALO = 16

TOK_TILE = 512
LN_TILE = 1024
ATT_TQ = 2048
ATT_TK = 1024
ATT_HEADS_PER_STEP = 1
ROUTE_TILE = 512
DISPATCH_TILE = 1024
COMBINE_TILE = 256
EXPERT_BLOCK = 512
FFN_CHUNK = 256

NEG_BIG = -1e30
_NT = (((1,), (1,)), ((), ()))
_TN = (((0,), (0,)), ((), ()))


def _layer_norm(y, g, b):
    mu = jnp.mean(y, axis=-1, keepdims=True)
    yc = y - mu
    var = jnp.mean(yc * yc, axis=-1, keepdims=True)
    return yc * lax.rsqrt(var + LN_EPS) * g + b


def _mla_proj_kernel(x_ref, cost_ref, sint_ref, cosp_ref, sinp_ref, wfm_ref, wtok_ref, wuq_ref,
                     wv_ref, wkn_ref, qn_ref, kvnc_ref, kvnr_ref, qt_ref, k_ref, vt_ref, *, q_scale):
    t = x_ref.shape[0]
    xb = x_ref[...].astype(BF16)
    lat_t = lax.dot_general(wfm_ref[...], xb, _NT, preferred_element_type=F32)
    cq = lat_t[0:Q_LORA]
    cq = cq * lax.rsqrt(jnp.mean(cq * cq, axis=0, keepdims=True) + RMS_EPS) * qn_ref[...]
    ckv = lat_t[Q_LORA:Q_LORA + KV_LORA]
    ckv = ckv * lax.rsqrt(jnp.mean(ckv * ckv, axis=0, keepdims=True) + RMS_EPS) * kvnc_ref[...]

    q_t = jnp.dot(wuq_ref[...], cq.astype(BF16), preferred_element_type=F32) * q_scale
    n_nope = N_HEADS * QK_NOPE
    n_half = N_HEADS * HALF_ROPE
    qn = q_t[0:n_nope].reshape(N_HEADS, QK_NOPE, t)
    x1 = q_t[n_nope:n_nope + n_half].reshape(N_HEADS, HALF_ROPE, t)
    x2 = q_t[n_nope + n_half:n_nope + 2 * n_half].reshape(N_HEADS, HALF_ROPE, t)
    c = cost_ref[...]
    s = sint_ref[...]
    qt_ref[0, :, 0:QK_NOPE, :] = qn.astype(BF16)
    qt_ref[0, :, QK_NOPE:QK_NOPE + HALF_ROPE, :] = (x1 * c - x2 * s).astype(BF16)
    qt_ref[0, :, QK_NOPE + HALF_ROPE:QK_NOPE + QK_ROPE, :] = (x2 * c + x1 * s).astype(BF16)
    qt_ref[0, :, QK_NOPE + QK_ROPE:HEAD_PAD, :] = jnp.zeros(
        (N_HEADS, HEAD_PAD - QK_NOPE - QK_ROPE, t), BF16)

    v_t = jnp.dot(wv_ref[...], ckv.astype(BF16), preferred_element_type=F32)
    vt_ref[0, :, 0:V_DIM, :] = v_t.reshape(N_HEADS, V_DIM, t).astype(BF16)
    vt_ref[0, :, V_DIM:V_ROWS, :] = jnp.ones((N_HEADS, V_ROWS - V_DIM, t), BF16)

    lat = jnp.dot(xb, wtok_ref[...], preferred_element_type=F32)
    ckv_tok = lat[:, 0:KV_LORA]
    ckv_tok = ckv_tok * lax.rsqrt(jnp.mean(ckv_tok * ckv_tok, axis=-1, keepdims=True) + RMS_EPS) * kvnr_ref[...]
    k_rope = lat[:, KV_LORA:KV_LORA + LANES] * cosp_ref[...] + lat[:, KV_LORA + LANES:] * sinp_ref[...]
    kn = jnp.dot(ckv_tok.astype(BF16), wkn_ref[...], preferred_element_type=F32)
    for h in range(N_HEADS):
        k_ref[0, h] = (kn[:, h * HEAD_PAD:(h + 1) * HEAD_PAD] + k_rope).astype(BF16)


def _mla_proj(x2d, tabs, w, batch, seq):
    n, d = x2d.shape
    t = TOK_TILE
    tiles_per_seq = seq // t
    q_scale = (QK_NOPE + QK_ROPE) ** -0.5 * math.log2(math.e)
    full = lambda a: pl.BlockSpec(a.shape, lambda i: (0,) * a.ndim)
    bs_idx = lambda i: (i // tiles_per_seq, 0, 0, i % tiles_per_seq)
    return pl.pallas_call(
        functools.partial(_mla_proj_kernel, q_scale=q_scale),
        grid=(n // t,),
        in_specs=[
            pl.BlockSpec((t, d), lambda i: (i, 0)),
            pl.BlockSpec((1, HALF_ROPE, t), lambda i: (i // tiles_per_seq, 0, i % tiles_per_seq)),
            pl.BlockSpec((1, HALF_ROPE, t), lambda i: (i // tiles_per_seq, 0, i % tiles_per_seq)),
            pl.BlockSpec((t, LANES), lambda i: (i, 0)),
            pl.BlockSpec((t, LANES), lambda i: (i, 0)),
            full(w['w_fm']), full(w['w_tok']), full(w['w_uq_t']), full(w['w_v_t']), full(w['w_kn']),
            full(w['qn_col']), full(w['kvn_col']), full(w['kvn_row']),
        ],
        out_specs=[
            pl.BlockSpec((1, N_HEADS, HEAD_PAD, t), bs_idx),
            pl.BlockSpec((1, N_HEADS, t, HEAD_PAD), lambda i: (i // tiles_per_seq, 0, i % tiles_per_seq, 0)),
            pl.BlockSpec((1, N_HEADS, V_ROWS, t), bs_idx),
        ],
        out_shape=[
            jax.ShapeDtypeStruct((batch, N_HEADS, HEAD_PAD, seq), BF16),
            jax.ShapeDtypeStruct((batch, N_HEADS, seq, HEAD_PAD), BF16),
            jax.ShapeDtypeStruct((batch, N_HEADS, V_ROWS, seq), BF16),
        ],
        compiler_params=pltpu.CompilerParams(dimension_semantics=("arbitrary",),
                                             vmem_limit_bytes=56 * 1024 * 1024),
        name="mla_proj",
    )(x2d, tabs['cos_t'], tabs['sin_t'], tabs['cos_pad'], tabs['sin_pad'],
      w['w_fm'], w['w_tok'], w['w_uq_t'], w['w_v_t'], w['w_kn'], w['qn_col'], w['kvn_col'], w['kvn_row'])


def _attn_kernel(qt_ref, k_ref, vt_ref, o_ref, s_scr, p_scr):
    n_h = qt_ref.shape[1]
    seq = k_ref.shape[2]
    tq, tk = ATT_TQ, ATT_TK
    n_q = seq // tq
    heads = range(n_h)

    def load_q(q0):
        return [q0 for _ in heads]

    def scores(h, q_at, k0, slot):
        kb = k_ref[0, h, pl.ds(k0, tk), :]
        q_t = qt_ref[0, h, :, pl.ds(q_at, tq)]
        s = jnp.dot(kb, q_t, preferred_element_type=F32)
        s_scr[h, slot] = s
        return jnp.max(s, axis=0, keepdims=True)

    def softmax(h, slot, m, bmax):
        s = s_scr[h, slot]
        m_new = jnp.maximum(m, bmax)
        p_scr[h, slot] = jnp.exp2(s - m_new).astype(BF16)
        return m_new, jnp.exp2(m - m_new)

    def values(h, k0, slot, alpha, acc):
        vb = vt_ref[0, h, :, pl.ds(k0, tk)]
        return alpha * acc + jnp.dot(vb, p_scr[h, slot], preferred_element_type=F32)

    def trip(t, carry, q_ts):
        m, acc, a_pend, bmax0 = [list(c) for c in carry]
        ka = pl.multiple_of(t * tq, tq)
        k_prev = pl.multiple_of(jnp.maximum(ka - tk, 0), tk)
        a0, bmax1 = [None] * n_h, [None] * n_h
        for h in heads:
            bmax1[h] = scores(h, q_ts[h], ka + tk, 1)
        for h in heads:
            m[h], a0[h] = softmax(h, 0, m[h], bmax0[h])
        for h in heads:
            acc[h] = values(h, k_prev, 1, a_pend[h], acc[h])
        for h in heads:
            bmax0[h] = scores(h, q_ts[h], ka + tq, 0)
        for h in heads:
            m[h], a_pend[h] = softmax(h, 1, m[h], bmax1[h])
        for h in heads:
            acc[h] = values(h, ka, 0, a0[h], acc[h])
        return tuple(m), tuple(acc), tuple(a_pend), tuple(bmax0)

    def diagonal_trip(q0, carry, q_ts, q_next):
        m, acc, a_pend, _ = carry
        k_prev = pl.multiple_of(jnp.maximum(q0 - tk, 0), tk)
        tri = lax.broadcasted_iota(I32, (tk, tk), 0) <= lax.broadcasted_iota(I32, (tk, tk), 1)
        bmax_next = []
        for h in heads:
            kb = k_ref[0, h, pl.ds(q0 + tk, tk), :]
            q_right = qt_ref[0, h, :, pl.ds(pl.multiple_of(q_ts[h] + tk, tk), tk)]
            s_b = jnp.dot(kb, q_right, preferred_element_type=F32)
            s_a = s_scr[h, 0]
            s_al = jnp.where(tri, s_a[:, :tk], NEG_BIG)
            s_ar = s_a[:, tk:]
            bmax_a = jnp.concatenate([jnp.max(s_al, axis=0, keepdims=True),
                                      jnp.max(s_ar, axis=0, keepdims=True)], axis=1)
            m_a = jnp.maximum(m[h], bmax_a)
            a0 = jnp.exp2(m[h] - m_a)
            p_scr[h, 0] = jnp.concatenate([jnp.exp2(s_al - m_a[:, :tk]), jnp.exp2(s_ar - m_a[:, tk:])],
                                          axis=1).astype(BF16)
            acc_h = values(h, k_prev, 1, a_pend[h], acc[h])
            bmax_next.append(scores(h, q_next[h], 0, 0))
            s_b = jnp.where(tri, s_b, NEG_BIG)
            m_r = jnp.maximum(m_a[:, tk:], jnp.max(s_b, axis=0, keepdims=True))
            p_b = jnp.exp2(s_b - m_r).astype(BF16)
            acc_h = values(h, q0, 0, a0, acc_h)
            vb = vt_ref[0, h, :, pl.ds(q0 + tk, tk)]
            acc_r = jnp.exp2(m_a[:, tk:] - m_r) * acc_h[:, tk:] + jnp.dot(vb, p_b, preferred_element_type=F32)
            out = jnp.concatenate([acc_h[:, :tk], acc_r], axis=1)
            o_ref[0, h * V_DIM:(h + 1) * V_DIM, pl.ds(q0, tq)] = (
                out[0:V_DIM] / out[V_DIM:V_DIM + 1]).astype(BF16)
        return tuple(bmax_next)

    def q_tile(qi, bmax0):
        q0 = pl.multiple_of(qi * tq, tq)
        q_ts = load_q(q0)
        q_next = load_q(pl.multiple_of(jnp.minimum(q0 + tq, seq - tq), tq))
        for h in heads:
            p_scr[h, 1] = jnp.zeros((tk, tq), BF16)
        init = (tuple(jnp.full((1, tq), NEG_BIG, F32) for _ in heads),
                tuple(jnp.zeros((V_ROWS, tq), F32) for _ in heads),
                tuple(jnp.ones((1, tq), F32) for _ in heads),
                bmax0)
        res = lax.fori_loop(0, qi, lambda t, c: trip(t, c, q_ts), init)
        return diagonal_trip(q0, res, q_ts, q_next)

    q_first = load_q(0)
    lax.fori_loop(0, n_q, q_tile, tuple(scores(h, q_first[h], 0, 0) for h in heads))


def _attention(q_t, k, v_t):
    batch, heads, _, seq = q_t.shape
    g = ATT_HEADS_PER_STEP
    assert ATT_TQ == 2 * ATT_TK and seq % ATT_TQ == 0 and heads % g == 0
    return pl.pallas_call(
        _attn_kernel,
        grid=(batch, heads // g),
        in_specs=[
            pl.BlockSpec((1, g, HEAD_PAD, seq), lambda b, h: (b, h, 0, 0)),
            pl.BlockSpec((1, g, seq, HEAD_PAD), lambda b, h: (b, h, 0, 0)),
            pl.BlockSpec((1, g, V_ROWS, seq), lambda b, h: (b, h, 0, 0)),
        ],
        out_specs=pl.BlockSpec((1, g * V_DIM, seq), lambda b, h: (b, h, 0)),
        out_shape=jax.ShapeDtypeStruct((batch, heads * V_DIM, seq), BF16),
        scratch_shapes=[pltpu.VMEM((g, 2, ATT_TK, ATT_TQ), F32), pltpu.VMEM((g, 2, ATT_TK, ATT_TQ), BF16)],
        compiler_params=pltpu.CompilerParams(dimension_semantics=("arbitrary", "arbitrary"),
                                             vmem_limit_bytes=56 * 1024 * 1024),
        name="mla_attention",
    )(q_t, k, v_t)


def _outproj_ln_kernel(ot_ref, x_ref, wo_ref, g_ref, b_ref, out_ref, *, alpha):
    f = lax.dot_general(ot_ref[0], wo_ref[...], _TN, preferred_element_type=F32)
    out_ref[...] = _layer_norm(alpha * x_ref[...] + f, g_ref[...], b_ref[...])


def _outproj_ln(o_t, x2d, w_o, g, b, alpha):
    n, d = x2d.shape
    batch, hd, seq = o_t.shape
    t = LN_TILE
    tiles_per_seq = seq // t
    return pl.pallas_call(
        functools.partial(_outproj_ln_kernel, alpha=alpha),
        grid=(n // t,),
        in_specs=[
            pl.BlockSpec((1, hd, t), lambda i: (i // tiles_per_seq, 0, i % tiles_per_seq)),
            pl.BlockSpec((t, d), lambda i: (i, 0)),
            pl.BlockSpec((hd, d), lambda i: (0, 0)),
            pl.BlockSpec((1, d), lambda i: (0, 0)),
            pl.BlockSpec((1, d), lambda i: (0, 0)),
        ],
        out_specs=pl.BlockSpec((t, d), lambda i: (i, 0)),
        out_shape=jax.ShapeDtypeStruct((n, d), F32),
        compiler_params=pltpu.CompilerParams(dimension_semantics=("arbitrary",)),
        name="mla_outproj_ln",
    )(o_t, x2d, w_o, g, b)


def _pool_ln_kernel(x_ref, halo_ref, w_ref, pb_ref, ps_ref, g_ref, b_ref, out_ref, *, alpha, tiles_per_seq):
    t, d = x_ref.shape
    gd = d // len(POOL_WINDOWS)
    i = pl.program_id(0)
    tile_in_seq = i % tiles_per_seq
    x = x_ref[...]
    halo = jnp.where(tile_in_seq == 0, 0.0, halo_ref[...])
    ext = jnp.concatenate([halo, x], axis=0)
    pos = tile_in_seq * t + lax.broadcasted_iota(I32, (t, 1), 0)
    ys = []
    for gi, win in enumerate(POOL_WINDOWS):
        e = ext[:, gi * gd:(gi + 1) * gd]
        sh = 1
        while sh < win:
            e = e + pltpu.roll(e, sh, axis=0)
            sh *= 2
        inv_cnt = 1.0 / jnp.minimum(pos + 1, win).astype(F32)
        pooled = e[POOL_HALO:] * inv_cnt - x[:, gi * gd:(gi + 1) * gd]
        ys.append(jnp.dot(pooled.astype(BF16), w_ref[gi], preferred_element_type=F32))
    y = (jnp.concatenate(ys, axis=1) + pb_ref[...]) * ps_ref[...]
    out_ref[...] = _layer_norm(alpha * x + y, g_ref[...], b_ref[...])


def _pool_ln(x2d, seq, w, pb, ps, g, b, alpha):
    n, d = x2d.shape
    t = LN_TILE
    tiles_per_seq = seq // t
    halo_blocks = t // POOL_HALO
    vec = pl.BlockSpec((1, d), lambda i: (0, 0))
    return pl.pallas_call(
        functools.partial(_pool_ln_kernel, alpha=alpha, tiles_per_seq=tiles_per_seq),
        grid=(n // t,),
        in_specs=[
            pl.BlockSpec((t, d), lambda i: (i, 0)),
            pl.BlockSpec((POOL_HALO, d), lambda i: (jnp.maximum(i * halo_blocks - 1, 0), 0)),
            pl.BlockSpec(w.shape, lambda i: (0, 0, 0)),
            vec, vec, vec, vec,
        ],
        out_specs=pl.BlockSpec((t, d), lambda i: (i, 0)),
        out_shape=jax.ShapeDtypeStruct((n, d), F32),
        compiler_params=pltpu.CompilerParams(dimension_semantics=("arbitrary",)),
        name="pool_ln",
    )(x2d, x2d, w, pb, ps, g, b)


def _router_kernel(x_ref, whi_ref, wlo_ref, bias_ref, tri_ref, eid_ref, gate_ref, rank_ref, cnt_ref, run_ref):
    t = x_ref.shape[0]

    @pl.when(pl.program_id(0) == 0)
    def _():
        run_ref[...] = jnp.zeros_like(run_ref)

    x = x_ref[...]
    xh = x.astype(BF16)
    xl = (x - xh.astype(F32)).astype(BF16)
    lg = (lax.dot_general(whi_ref[...], xh, _NT, preferred_element_type=F32)
          + lax.dot_general(whi_ref[...], xl, _NT, preferred_element_type=F32)
          + lax.dot_general(wlo_ref[...], xh, _NT, preferred_element_type=F32)
          + bias_ref[...])
    ridx = lax.broadcasted_iota(I32, (N_GROUPS, t), 0)

    g = lg[0:N_GROUPS]
    gmax = jnp.max(g, axis=0, keepdims=True)
    g_sel = jnp.min(jnp.where(g == gmax, ridx, N_GROUPS), axis=0, keepdims=True)
    g_gate = 1.0 / jnp.sum(jnp.exp(g - gmax), axis=0, keepdims=True)

    e_all = lg[N_GROUPS:N_GROUPS + N_EXPERTS]
    e = jnp.zeros((EXP_PER_GROUP, t), F32)
    for gi in range(N_GROUPS):
        e = e + jnp.where(g_sel == gi, e_all[gi * EXP_PER_GROUP:(gi + 1) * EXP_PER_GROUP], 0.0)
    pe = jnp.exp(e - jnp.max(e, axis=0, keepdims=True))
    p1 = jnp.max(pe, axis=0, keepdims=True)
    i1 = jnp.min(jnp.where(pe == p1, ridx, EXP_PER_GROUP), axis=0, keepdims=True)
    pe2 = jnp.where(ridx == i1, -1.0, pe)
    p2 = jnp.max(pe2, axis=0, keepdims=True)
    i2 = jnp.min(jnp.where(pe2 == p2, ridx, EXP_PER_GROUP), axis=0, keepdims=True)
    denom = p1 + p2
    e1 = g_sel * EXP_PER_GROUP + i1
    e2 = g_sel * EXP_PER_GROUP + i2
    eid_ref[0:1, :] = e1
    eid_ref[1:2, :] = e2
    gate_ref[0:1, :] = g_gate * p1 / denom
    gate_ref[1:2, :] = g_gate * p2 / denom

    eidx = lax.broadcasted_iota(I32, (N_EXPERTS, t), 0)
    oh1 = eidx == e1
    oh2 = eidx == e2
    oh = jnp.where(oh1 | oh2, 1.0, 0.0)
    earlier = jnp.dot(oh.astype(BF16), tri_ref[...], preferred_element_type=F32)
    tot = earlier + run_ref[:, 0:1]
    rank_ref[0:1, :] = jnp.sum(jnp.where(oh1, tot, 0.0), axis=0, keepdims=True).astype(I32)
    rank_ref[1:2, :] = jnp.sum(jnp.where(oh2, tot, 0.0), axis=0, keepdims=True).astype(I32)
    run_ref[...] = run_ref[...] + jnp.sum(oh, axis=1, keepdims=True)
    cnt_ref[...] = run_ref[...].astype(I32)


def _router(x2d, w_hi, w_lo, bias, tri):
    n, d = x2d.shape
    t = ROUTE_TILE
    tok = lambda dt: jax.ShapeDtypeStruct((TOP_K, n), dt)
    tok_spec = pl.BlockSpec((TOP_K, t), lambda i: (0, i))
    return pl.pallas_call(
        _router_kernel,
        grid=(n // t,),
        in_specs=[
            pl.BlockSpec((t, d), lambda i: (i, 0)),
            pl.BlockSpec(w_hi.shape, lambda i: (0, 0)),
            pl.BlockSpec(w_lo.shape, lambda i: (0, 0)),
            pl.BlockSpec(bias.shape, lambda i: (0, 0)),
            pl.BlockSpec(tri.shape, lambda i: (0, 0)),
        ],
        out_specs=[tok_spec, tok_spec, tok_spec, pl.BlockSpec((N_EXPERTS, LANES), lambda i: (0, 0))],
        out_shape=[tok(I32), tok(F32), tok(I32), jax.ShapeDtypeStruct((N_EXPERTS, LANES), I32)],
        scratch_shapes=[pltpu.VMEM((N_EXPERTS, LANES), F32)],
        compiler_params=pltpu.CompilerParams(dimension_semantics=("arbitrary",)),
        name="moe_router",
    )(x2d, w_hi, w_lo, bias, tri)


ROWS_PER_TRIP = 8


def _pack_halves(y):
    half = y.shape[1] // 2
    bits = lax.bitcast_convert_type(y.astype(BF16).astype(F32), jnp.uint32)
    return bits[:, :half] | (bits[:, half:] >> 16)


def _unpack_halves(p):
    hi = lax.bitcast_convert_type(p & jnp.uint32(0xFFFF0000), F32)
    lo = lax.bitcast_convert_type(p << 16, F32)
    return hi, lo


def _for_row_groups(n_rows, body):
    def trip(g, c):
        base = g * ROWS_PER_TRIP
        for u in range(ROWS_PER_TRIP):
            body(g, u, base + u)
        return c
    lax.fori_loop(0, n_rows // ROWS_PER_TRIP, trip, 0)


def _staged_row(buf, g, u):
    return buf.at[g, pl.ds(u, 1)]


def _hbm_row(arr, row):
    return arr.at[pl.ds(row, 1)]


def _slots_kernel(pstart_ref, eid_ref, rank_ref, dest_ref):
    eid = eid_ref[...]

    def add_expert(e, acc):
        return acc + jnp.where(eid == e, pstart_ref[e], 0)

    dest_ref[...] = lax.fori_loop(0, N_EXPERTS, add_expert, rank_ref[...])


def _slots(pstart, eid, rank):
    full = pl.BlockSpec(eid.shape, lambda i, ps: (0, 0))
    return pl.pallas_call(
        _slots_kernel,
        grid_spec=pltpu.PrefetchScalarGridSpec(num_scalar_prefetch=1, grid=(1,),
                                               in_specs=[full, full], out_specs=full),
        out_shape=jax.ShapeDtypeStruct(eid.shape, I32),
        name="moe_slots",
    )(pstart, eid, rank)


def _dispatch_kernel(d0_ref, d1_ref, x_ref, rows_in_hbm, rows_hbm, xp_scr, sem):
    del rows_in_hbm
    t = x_ref.shape[0]
    xp_scr[...] = _pack_halves(x_ref[...]).reshape(xp_scr.shape)

    def issue(g, u, r):
        for k, d_ref in enumerate((d0_ref, d1_ref)):
            pltpu.make_async_copy(_staged_row(xp_scr, g, u), _hbm_row(rows_hbm, d_ref[r]), sem).start(priority=k)

    def drain(g, u, r):
        for _ in range(TOP_K):
            pltpu.make_async_copy(_staged_row(xp_scr, 0, 0), _hbm_row(rows_hbm, 0), sem).wait()

    _for_row_groups(t, issue)
    _for_row_groups(t, drain)


def _dispatch(dest, x2d, rows_init):
    n, d = x2d.shape
    t = DISPATCH_TILE
    tok_spec = pl.BlockSpec((t,), lambda i: (i,), memory_space=pltpu.SMEM)
    return pl.pallas_call(
        _dispatch_kernel,
        grid=(n // t,),
        in_specs=[
            tok_spec, tok_spec,
            pl.BlockSpec((t, d), lambda i: (i, 0)),
            pl.BlockSpec(memory_space=pl.ANY),
        ],
        out_specs=pl.BlockSpec(memory_space=pl.ANY),
        out_shape=jax.ShapeDtypeStruct(rows_init.shape, rows_init.dtype),
        scratch_shapes=[pltpu.VMEM((t // ROWS_PER_TRIP, ROWS_PER_TRIP, d // 2), jnp.uint32),
                        pltpu.SemaphoreType.DMA(())],
        input_output_aliases={3: 0},
        compiler_params=pltpu.CompilerParams(dimension_semantics=("arbitrary",)),
        name="moe_dispatch",
    )(dest[0], dest[1], x2d, rows_init)


def _ffn_kernel(be_ref, nu_ref, rows_ref, w1_ref, w3_ref, w2_ref, y_ref, w1b, w3b, w2b):
    b = pl.program_id(0)
    half = rows_ref.shape[1]

    @pl.when(b < nu_ref[0])
    def _():
        prev = be_ref[jnp.maximum(b - 1, 0)]

        @pl.when((b == 0) | (be_ref[b] != prev))
        def _():
            w1b[...] = w1_ref[0, 0].astype(BF16)
            w3b[...] = w3_ref[0, 0].astype(BF16)
            w2b[...] = w2_ref[0, 0].astype(BF16)

        for c in range(rows_ref.shape[0] // FFN_CHUNK):
            rows = pl.ds(c * FFN_CHUNK, FFN_CHUNK)
            x_hi, x_lo = _unpack_halves(rows_ref[rows, :])
            x_hi, x_lo = x_hi.astype(BF16), x_lo.astype(BF16)

            def up(w):
                return (jnp.dot(x_hi, w[0:half, :], preferred_element_type=F32)
                        + jnp.dot(x_lo, w[half:2 * half, :], preferred_element_type=F32))

            h1 = up(w1b)
            hb = h1 * jax.nn.sigmoid(h1) * up(w3b)
            y_ref[rows, :] = _pack_halves(jnp.dot(hb.astype(BF16), w2b[...], preferred_element_type=F32))


def _expert_ffn(blk_e, n_used, rows, w1, w3, w2, layer):
    r, half = rows.shape
    d, de = w1.shape[2], w1.shape[3]
    bm = EXPERT_BLOCK
    nb = r // bm
    clamp = lambda b, nu: jnp.minimum(b, nu[0] - 1)
    row_spec = pl.BlockSpec((bm, half), lambda b, be, nu: (clamp(b, nu), 0))
    w_idx = lambda b, be, nu: (layer, be[clamp(b, nu)], 0, 0)
    return pl.pallas_call(
        _ffn_kernel,
        grid_spec=pltpu.PrefetchScalarGridSpec(
            num_scalar_prefetch=2,
            grid=(nb,),
            in_specs=[
                row_spec,
                pl.BlockSpec((1, 1, d, de), w_idx),
                pl.BlockSpec((1, 1, d, de), w_idx),
                pl.BlockSpec((1, 1, de, d), w_idx),
            ],
            out_specs=row_spec,
            scratch_shapes=[pltpu.VMEM((d, de), BF16), pltpu.VMEM((d, de), BF16), pltpu.VMEM((de, d), BF16)],
        ),
        out_shape=jax.ShapeDtypeStruct((r, half), jnp.uint32),
        compiler_params=pltpu.CompilerParams(dimension_semantics=("arbitrary",)),
        name="moe_expert_ffn",
    )(blk_e, n_used, rows, w1, w3, w2)


def _combine_ln_kernel(d0_ref, d1_ref, n0_ref, n1_ref, gate_ref, x_ref, y_hbm, g_ref, b_ref, out_ref,
                       ybuf, sems, *, alpha):
    t = COMBINE_TILE
    i = pl.program_id(0)

    def request_row(slot, d_refs, row, g, u):
        for k, d_ref in enumerate(d_refs):
            pltpu.make_async_copy(_hbm_row(y_hbm, d_ref[row]), _staged_row(ybuf.at[slot, k], g, u),
                                  sems.at[slot]).start(priority=k)

    def wait_tile(slot):
        def wait(g, u, r):
            for k in range(TOP_K):
                pltpu.make_async_copy(_hbm_row(y_hbm, 0), _staged_row(ybuf.at[slot, k], 0, 0), sems.at[slot]).wait()
        _for_row_groups(t, wait)

    def request_tile(slot, d_refs, first_row):
        _for_row_groups(t, lambda g, u, r: request_row(slot, d_refs, first_row + r, g, u))

    def reduce_tile(slot, first_row):
        wait_tile(slot)
        rows = pl.ds(first_row, t)
        gate = gate_ref[rows, :]
        hi0, lo0 = _unpack_halves(ybuf[slot, 0].reshape(t, -1))
        hi1, lo1 = _unpack_halves(ybuf[slot, 1].reshape(t, -1))
        g0, g1 = gate[:, 0:1], gate[:, 1:2]
        f = jnp.concatenate([g0 * hi0 + g1 * hi1, g0 * lo0 + g1 * lo1], axis=1)
        out_ref[rows, :] = _layer_norm(alpha * x_ref[rows, :] + f, g_ref[...], b_ref[...])

    @pl.when(i == 0)
    def _():
        request_tile(0, (d0_ref, d1_ref), 0)

    request_tile(1, (d0_ref, d1_ref), t)
    reduce_tile(0, 0)

    @pl.when(i + 1 < pl.num_programs(0))
    def _():
        request_tile(0, (n0_ref, n1_ref), 0)

    reduce_tile(1, t)


def _combine_ln(dest, gates_tok, x2d, y_rows, g, b, alpha):
    n, d = x2d.shape
    t = COMBINE_TILE
    n_tiles = n // t
    vec = pl.BlockSpec((1, d), lambda i: (0, 0))
    pair_spec = pl.BlockSpec((2 * t,), lambda i: (i,), memory_space=pltpu.SMEM)
    next_spec = pl.BlockSpec((t,), lambda i: (jnp.minimum(2 * i + 2, n_tiles - 1),), memory_space=pltpu.SMEM)
    return pl.pallas_call(
        functools.partial(_combine_ln_kernel, alpha=alpha),
        grid=(n_tiles // 2,),
        in_specs=[
            pair_spec, pair_spec, next_spec, next_spec,
            pl.BlockSpec((2 * t, TOP_K), lambda i: (i, 0)),
            pl.BlockSpec((2 * t, d), lambda i: (i, 0)),
            pl.BlockSpec(memory_space=pl.ANY),
            vec, vec,
        ],
        out_specs=pl.BlockSpec((2 * t, d), lambda i: (i, 0)),
        out_shape=jax.ShapeDtypeStruct((n, d), F32),
        scratch_shapes=[pltpu.VMEM((2, TOP_K, t // ROWS_PER_TRIP, ROWS_PER_TRIP, d // 2), jnp.uint32),
                        pltpu.SemaphoreType.DMA((2,))],
        compiler_params=pltpu.CompilerParams(dimension_semantics=("arbitrary",)),
        name="moe_combine_ln",
    )(dest[0], dest[1], dest[0], dest[1], gates_tok, x2d, y_rows, g, b)


def _rope_tables(positions):
    inv_freq = ROPE_THETA ** (-jnp.arange(0, QK_ROPE, 2, dtype=F32) / QK_ROPE)
    ang = positions.astype(F32)[..., None] * inv_freq
    cos, sin = jnp.cos(ang), jnp.sin(ang)
    b, s, _ = cos.shape
    zeros = jnp.zeros((b * s, QK_NOPE), F32)
    tail = jnp.zeros((b * s, LANES - QK_NOPE - QK_ROPE), F32)
    c2, s2 = cos.reshape(b * s, -1), sin.reshape(b * s, -1)
    return {
        'cos_t': jnp.swapaxes(cos, 1, 2), 'sin_t': jnp.swapaxes(sin, 1, 2),
        'cos_pad': jnp.concatenate([zeros, c2, c2, tail], axis=1),
        'sin_pad': jnp.concatenate([zeros, s2, s2, tail], axis=1),
    }


def _mla_weights(w_in, q_norm, w_uq, kv_norm, w_ukv, w_o):
    d = w_in.shape[0]
    lat_q_kv = Q_LORA + KV_LORA
    r1 = w_in[:, lat_q_kv:lat_q_kv + HALF_ROPE]
    r2 = w_in[:, lat_q_kv + HALF_ROPE:lat_q_kv + QK_ROPE]
    z_lo = jnp.zeros((d, QK_NOPE), F32)
    z_hi = jnp.zeros((d, LANES - QK_NOPE - QK_ROPE), F32)
    w_tok = jnp.concatenate([w_in[:, Q_LORA:lat_q_kv],
                             z_lo, r1, r2, z_hi,
                             z_lo, -r2, r1, z_hi],
                            axis=1)
    uq = w_uq.reshape(Q_LORA, N_HEADS, QK_NOPE + QK_ROPE)
    w_uq_t = jnp.concatenate([uq[:, :, :QK_NOPE].reshape(Q_LORA, -1),
                              uq[:, :, QK_NOPE:QK_NOPE + HALF_ROPE].reshape(Q_LORA, -1),
                              uq[:, :, QK_NOPE + HALF_ROPE:].reshape(Q_LORA, -1)], axis=1).T
    ukv = w_ukv.reshape(KV_LORA, N_HEADS, QK_NOPE + V_DIM)
    w_kn = jnp.concatenate([ukv[:, :, :QK_NOPE], jnp.zeros((KV_LORA, N_HEADS, HEAD_PAD - QK_NOPE), F32)],
                           axis=2).reshape(KV_LORA, N_HEADS * HEAD_PAD)
    w_v_t = ukv[:, :, QK_NOPE:].reshape(KV_LORA, N_HEADS * V_DIM).T
    return {
        'w_fm': w_in[:, :lat_q_kv].T.astype(BF16), 'w_tok': w_tok.astype(BF16),
        'w_uq_t': w_uq_t.astype(BF16), 'w_v_t': w_v_t.astype(BF16), 'w_kn': w_kn.astype(BF16),
        'qn_col': q_norm.reshape(-1, 1), 'kvn_col': kv_norm.reshape(-1, 1), 'kvn_row': kv_norm.reshape(1, -1),
        'w_o': w_o.astype(BF16),
    }


def _router_weights(w_grp, b_grp, w_exp, b_exp):
    d = w_grp.shape[0]
    pad = LANES - N_GROUPS - N_EXPERTS
    w = jnp.concatenate([w_grp, w_exp, jnp.zeros((d, pad), F32)], axis=1).T
    w_hi = w.astype(BF16)
    w_lo = (w - w_hi.astype(F32)).astype(BF16)
    bias = jnp.concatenate([b_grp, b_exp, jnp.zeros((pad,), F32)]).reshape(-1, 1)
    return w_hi, w_lo, bias


def _moe_ln(x2d, w_hi, w_lo, bias, tri, w1, w3, w2, layer, g, b, alpha, rows_buf):
    n, d = x2d.shape
    bm = EXPERT_BLOCK
    n_assign = n * TOP_K
    nb = -(-(n_assign + N_EXPERTS * (bm - 1)) // bm)
    if rows_buf is None:
        rows_buf = jnp.zeros((nb * bm, d // 2), jnp.uint32)
    eid, gates, rank, cnt = _router(x2d, w_hi, w_lo, bias, tri)
    counts = cnt[:, 0]
    padded = (counts + bm - 1) // bm * bm
    pend = jnp.cumsum(padded)
    pstart = (pend - padded).astype(I32)
    n_used = (pend[-1:] // bm).astype(I32)
    blk_start = jnp.arange(nb, dtype=I32) * bm
    blk_e = jnp.minimum(jnp.sum((pend[None, :] <= blk_start[:, None]).astype(I32), axis=1), N_EXPERTS - 1)
    dest = _slots(pstart, eid, rank)
    rows = _dispatch(dest, x2d, rows_buf)
    y_rows = _expert_ffn(blk_e, n_used, rows, w1, w3, w2, layer)
    return _combine_ln(dest, gates.T, x2d, y_rows, g, b, alpha), rows


def kernel(x, positions, ln_g, ln_b, mla_w_in, mla_q_norm, mla_w_uq, mla_kv_norm, mla_w_ukv, mla_w_o,
           pool_w, pool_b, pool_scale, moe_w_grp, moe_b_grp, moe_w_exp, moe_b_exp, moe_w1, moe_w3, moe_w2):
    batch, seq, d = x.shape
    depth = ln_g.shape[0]
    alpha = (2 * depth) ** 0.25
    tabs = _rope_tables(positions)
    t = ROUTE_TILE
    tri = (lax.broadcasted_iota(I32, (t, t), 0) < lax.broadcasted_iota(I32, (t, t), 1)).astype(BF16)
    x2d = x.reshape(batch * seq, d)
    vec = lambda v: v.reshape(1, d)
    rows_buf = None
    for i in range(depth):
        j = i // 2
        if i % 2 == 0:
            w = _mla_weights(mla_w_in[j], mla_q_norm[j], mla_w_uq[j], mla_kv_norm[j], mla_w_ukv[j], mla_w_o[j])
            q_t, k, v_t = _mla_proj(x2d, tabs, w, batch, seq)
            o_t = _attention(q_t, k, v_t)
            x2d = _outproj_ln(o_t, x2d, w['w_o'], vec(ln_g[i, 0]), vec(ln_b[i, 0]), alpha)
        else:
            x2d = _pool_ln(x2d, seq, pool_w[j].astype(BF16), vec(pool_b[j]), vec(pool_scale[j]),
                           vec(ln_g[i, 0]), vec(ln_b[i, 0]), alpha)
        w_hi, w_lo, bias = _router_weights(moe_w_grp[i], moe_b_grp[i], moe_w_exp[i], moe_b_exp[i])
        x2d, rows_buf = _moe_ln(x2d, w_hi, w_lo, bias, tri, moe_w1, moe_w3, moe_w2, i,
                                vec(ln_g[i, 1]), vec(ln_b[i, 1]), alpha, rows_buf)
    return x2d.reshape(batch, seq, d)
```
